```python
import math
import jax
import jax.numpy as jnp
from jax import lax
import numpy as np

D_MODEL = 2048
BATCH = 16
SEQ = 2048
DEPTH = 1
DEC_BATCH = 32
DEC_SEQ = 1
PAST_LEN = 16384
PAGE_SIZE = 128

ATTN_WIDTH = D_MODEL // 2
N_HEADS = 8
HEAD_DIM = ATTN_WIDTH // N_HEADS
SUB_DIM = HEAD_DIM // 2
ROT_DIM = SUB_DIM // 4
ROPE_THETA = 500000.0
SSM_WIDTH = D_MODEL - ATTN_WIDTH
SSM_GROUP = 16
N_SSM_GROUPS = SSM_WIDTH // SSM_GROUP
SSM_STATE = 64
IN_WIDTH = 3 * ATTN_WIDTH + SSM_WIDTH
N_EXPERTS = 32
TOP_K = 4
D_FF = D_MODEL
SWIGLU_LIMIT = 7.0
SWIGLU_ALPHA = 1.702
PLE_DIM = 256
Q_BLOCK = 128
NORM_EPS = 1e-6
SUBLN_EPS = 1e-5

kernel_name = 'hymba_diffattn_s5_moe_decode_step'


def rmsnorm(x, g, eps=NORM_EPS):
    xf = x.astype(jnp.float32)
    y = xf * lax.rsqrt(jnp.mean(xf * xf, axis=-1, keepdims=True) + eps) * g.astype(jnp.float32)
    return y.astype(x.dtype)


def rope(x, pos):
    inv = ROPE_THETA ** (-(jnp.arange(0, ROT_DIM, 2, dtype=jnp.float32) / ROT_DIM))
    ang = pos.astype(jnp.float32)[:, None] * inv[None, :]
    cos = jnp.cos(ang)[:, None, :]
    sin = jnp.sin(ang)[:, None, :]
    half = ROT_DIM // 2
    xr = x[..., :ROT_DIM].astype(jnp.float32)
    x1, x2 = xr[..., :half], xr[..., half:]
    rot = jnp.concatenate([x1 * cos - x2 * sin, x2 * cos + x1 * sin], axis=-1).astype(x.dtype)
    return jnp.concatenate([rot, x[..., ROT_DIM:]], axis=-1)


def diff_combine(p, lam):
    b, _, sq, sk = p.shape
    p = p.reshape(b, N_HEADS, 2, sq, sk)
    return p[:, :, 0] - lam * p[:, :, 1]


def diff_attn_prompt(q, k, v, lam):
    b, s = q.shape[:2]
    nb = s // Q_BLOCK
    scale = SUB_DIM ** -0.5
    qb = jnp.moveaxis(q.reshape(b, nb, Q_BLOCK, 2 * N_HEADS, SUB_DIM), 1, 0)
    kpos = jnp.arange(s)

    def one_block(args):
        q_blk, start = args
        qpos = start + jnp.arange(Q_BLOCK)
        mask = qpos[:, None] >= kpos[None, :]
        sc = jnp.einsum('bqhd,bkhd->bhqk', q_blk, k).astype(jnp.float32) * scale
        sc = jnp.where(mask, sc, -jnp.inf)
        a = diff_combine(jax.nn.softmax(sc, axis=-1), lam)
        return jnp.einsum('bhqk,bkhd->bqhd', a.astype(v.dtype), v)

    out = lax.map(one_block, (qb, jnp.arange(nb) * Q_BLOCK))
    return jnp.moveaxis(out, 0, 1).reshape(b, s, N_HEADS, HEAD_DIM)


def diff_attn_sample(q, k_new, v_new, k_past, v_past, lam):
    sq = q.shape[1]
    past = k_past.shape[1]
    scale = SUB_DIM ** -0.5
    s_past = jnp.einsum('bqhd,bkhd->bhqk', q, k_past).astype(jnp.float32) * scale
    s_new = jnp.einsum('bqhd,bkhd->bhqk', q, k_new).astype(jnp.float32) * scale
    causal = jnp.tril(jnp.ones((sq, sq), dtype=bool))
    s_new = jnp.where(causal, s_new, -jnp.inf)
    p = jax.nn.softmax(jnp.concatenate([s_past, s_new], axis=-1), axis=-1)
    a = diff_combine(p, lam).astype(v_new.dtype)
    return (jnp.einsum('bhqk,bkhd->bqhd', a[..., :past], v_past)
            + jnp.einsum('bhqk,bkhd->bqhd', a[..., past:], v_new))


def s5_discretize(a_re, a_im, log_dt, b_re, b_im):
    f32 = jnp.float32
    a_re, a_im = a_re.astype(f32), a_im.astype(f32)
    b_re, b_im = b_re.astype(f32), b_im.astype(f32)
    dt = jnp.exp(log_dt.astype(f32))[:, None]
    mag = jnp.exp(a_re * dt)
    ab_re = mag * jnp.cos(a_im * dt)
    ab_im = mag * jnp.sin(a_im * dt)
    nr, ni = ab_re - 1.0, ab_im
    den = a_re * a_re + a_im * a_im
    f_re = (nr * a_re + ni * a_im) / den
    f_im = (ni * a_re - nr * a_im) / den
    bb_re = f_re[..., None] * b_re - f_im[..., None] * b_im
    bb_im = f_re[..., None] * b_im + f_im[..., None] * b_re
    return ab_re, ab_im, bb_re, bb_im


def complex_affine_combine(e1, e2):
    a1r, a1i, b1r, b1i = e1
    a2r, a2i, b2r, b2i = e2
    return (a2r * a1r - a2i * a1i,
            a2r * a1i + a2i * a1r,
            a2r * b1r - a2i * b1i + b2r,
            a2r * b1i + a2i * b1r + b2i)


def s5_scan(u, ab_re, ab_im, bb_re, bb_im, c_re, c_im, d_skip, h0_re, h0_im):
    b, s, _ = u.shape
    f32 = jnp.float32
    ug = u.astype(f32).reshape(b, s, N_SSM_GROUPS, SSM_GROUP)
    bu_re = jnp.einsum('bsgh,gph->bsgp', ug, bb_re)
    bu_im = jnp.einsum('bsgh,gph->bsgp', ug, bb_im)
    a_re = jnp.broadcast_to(ab_re, (1, s) + ab_re.shape)
    a_im = jnp.broadcast_to(ab_im, (1, s) + ab_im.shape)
    acc_re, acc_im, x_re, x_im = lax.associative_scan(
        complex_affine_combine, (a_re, a_im, bu_re, bu_im), axis=1)
    if h0_re is not None:
        h0r = h0_re.astype(f32)[:, None]
        h0i = h0_im.astype(f32)[:, None]
        x_re, x_im = (x_re + acc_re * h0r - acc_im * h0i,
                      x_im + acc_re * h0i + acc_im * h0r)
    y = (jnp.einsum('bsgp,ghp->bsgh', x_re, c_re.astype(f32))
         - jnp.einsum('bsgp,ghp->bsgh', x_im, c_im.astype(f32))
         + d_skip.astype(f32) * ug)
    return y.reshape(b, s, SSM_WIDTH), x_re[:, -1], x_im[:, -1]


def s5_glu_out(y, w_glu, b_glu, g_out):
    z = jax.nn.gelu(y, approximate=False)
    out = z * jax.nn.sigmoid(z @ w_glu.astype(jnp.float32) + b_glu.astype(jnp.float32))
    return rmsnorm(out, g_out)


def moe(c, w_router, b_router, w_gu, b_gu, w_dn, b_dn):
    logits = (c @ w_router + b_router).astype(jnp.float32)
    top_v, top_i = lax.top_k(logits, TOP_K)
    top_w = jax.nn.softmax(top_v, axis=-1)
    gates = jnp.sum(jax.nn.one_hot(top_i, N_EXPERTS, dtype=jnp.float32) * top_w[..., None], axis=1)
    out = jnp.zeros(c.shape, jnp.float32)
    for e in range(N_EXPERTS):
        gu = c @ w_gu[e] + b_gu[e]
        gate = jnp.minimum(gu[..., :D_FF], SWIGLU_LIMIT)
        up = jnp.clip(gu[..., D_FF:], -SWIGLU_LIMIT, SWIGLU_LIMIT)
        hmid = (up + 1.0) * gate * jax.nn.sigmoid(SWIGLU_ALPHA * gate)
        out = out + gates[:, e:e + 1] * (hmid @ w_dn[e] + b_dn[e]).astype(jnp.float32)
    return out


def layer(h, p_l, pos, attend, h0_re, h0_im, lw):
    b, s, _ = h.shape
    a = rmsnorm(h, lw['g_mix'])
    proj = a @ lw['w_in']
    q = proj[..., :ATTN_WIDTH].reshape(b, s, 2 * N_HEADS, SUB_DIM)
    k = proj[..., ATTN_WIDTH:2 * ATTN_WIDTH].reshape(b, s, 2 * N_HEADS, SUB_DIM)
    v = proj[..., 2 * ATTN_WIDTH:3 * ATTN_WIDTH].reshape(b, s, N_HEADS, HEAD_DIM)
    u = proj[..., 3 * ATTN_WIDTH:]
    q = rope(q, pos)
    k = rope(k, pos)
    o = attend(q, k, v)
    o = (rmsnorm(o, lw['g_sub'], SUBLN_EPS) * (1.0 - lw['lam_init'])).reshape(b, s, ATTN_WIDTH)
    ab_re, ab_im, bb_re, bb_im = lw['s5']
    y_ssm, st_re, st_im = s5_scan(u, ab_re, ab_im, bb_re, bb_im, lw['c_re'], lw['c_im'],
                                  lw['d_skip'], h0_re, h0_im)
    m = s5_glu_out(y_ssm, lw['w_glu'], lw['b_glu'], lw['g_ssm_out'])
    mix = jnp.concatenate([o, m.astype(o.dtype)], axis=-1) @ lw['w_o']
    h = h + mix.astype(h.dtype)
    c = rmsnorm(h, lw['g_ffn']).reshape(b * s, D_MODEL)
    h = h + moe(c, lw['w_router'], lw['b_router'], lw['w_gu'], lw['b_gu'],
                lw['w_dn'], lw['b_dn']).reshape(b, s, D_MODEL).astype(h.dtype)
    gate = jax.nn.sigmoid((rmsnorm(h, lw['g_ple']) @ lw['w_ple_gate'] + lw['b_ple_gate']).astype(jnp.float32))
    h = h + (gate * (p_l @ lw['w_ple_proj']).astype(jnp.float32)).astype(h.dtype)
    return h, k, v, st_re, st_im


def setup_inputs(seed: int = 0) -> dict:
    key = jax.random.key(seed)
    ks = iter(jax.random.split(key, 48))
    f32 = jnp.float32

    def nrm(shape, scale):
        return jax.random.normal(next(ks), shape, f32) * scale

    n_pages = PAST_LEN // PAGE_SIZE
    n_used = DEC_BATCH * n_pages
    n_pool = n_used + max(1, n_used // 4)
    G, P, H = N_SSM_GROUPS, SSM_STATE, SSM_GROUP
    x_prompt = nrm((BATCH, SEQ, D_MODEL), 1.0)
    x_sample = nrm((DEC_BATCH, DEC_SEQ, D_MODEL), 1.0)
    p_prompt = nrm((DEPTH, BATCH, SEQ, PLE_DIM), 1.0)
    p_sample = nrm((DEPTH, DEC_BATCH, DEC_SEQ, PLE_DIM), 1.0)
    cache_k = nrm((DEPTH, n_pool, PAGE_SIZE, 2 * N_HEADS, SUB_DIM), 1.0)
    cache_v = nrm((DEPTH, n_pool, PAGE_SIZE, N_HEADS, HEAD_DIM), 1.0)
    state_ssm_re = nrm((DEPTH, DEC_BATCH, G, P), 0.5)
    state_ssm_im = nrm((DEPTH, DEC_BATCH, G, P), 0.5)
    page_table = jax.random.permutation(next(ks), n_pool)[:n_used].reshape(DEC_BATCH, n_pages).astype(jnp.int32)
    g_mix = 1.0 + nrm((DEPTH, D_MODEL), 0.01)
    w_in = nrm((DEPTH, D_MODEL, IN_WIDTH), D_MODEL ** -0.5)
    lambda_q1 = nrm((DEPTH, SUB_DIM), 0.1)
    lambda_k1 = nrm((DEPTH, SUB_DIM), 0.1)
    lambda_q2 = nrm((DEPTH, SUB_DIM), 0.1)
    lambda_k2 = nrm((DEPTH, SUB_DIM), 0.1)
    g_sub = 1.0 + nrm((DEPTH, HEAD_DIM), 0.01)
    a_re = -0.5 * jnp.exp(nrm((DEPTH, G, P), 0.01))
    a_im = math.pi * jnp.arange(P, dtype=f32) + nrm((DEPTH, G, P), 0.01)
    log_dt = jax.random.uniform(next(ks), (DEPTH, G), f32, math.log(1e-3), math.log(1e-1))
    b_re = nrm((DEPTH, G, P, H), (2.0 * H) ** -0.5)
    b_im = nrm((DEPTH, G, P, H), (2.0 * H) ** -0.5)
    c_re = nrm((DEPTH, G, H, P), P ** -0.5)
    c_im = nrm((DEPTH, G, H, P), P ** -0.5)
    d_skip = nrm((DEPTH, G, H), 1.0)
    w_glu = nrm((DEPTH, SSM_WIDTH, SSM_WIDTH), SSM_WIDTH ** -0.5)
    b_glu = nrm((DEPTH, SSM_WIDTH), 0.01)
    g_ssm_out = 1.0 + nrm((DEPTH, SSM_WIDTH), 0.01)
    w_o = nrm((DEPTH, D_MODEL, D_MODEL), D_MODEL ** -0.5)
    g_ffn = 1.0 + nrm((DEPTH, D_MODEL), 0.01)
    w_router = nrm((DEPTH, D_MODEL, N_EXPERTS), D_MODEL ** -0.5)
    b_router = nrm((DEPTH, N_EXPERTS), 0.01)
    w_gu = nrm((DEPTH, N_EXPERTS, D_MODEL, 2 * D_FF), D_MODEL ** -0.5)
    b_gu = nrm((DEPTH, N_EXPERTS, 2 * D_FF), 0.01)
    w_dn = nrm((DEPTH, N_EXPERTS, D_FF, D_MODEL), D_FF ** -0.5)
    b_dn = nrm((DEPTH, N_EXPERTS, D_MODEL), 0.01)
    g_ple = 1.0 + nrm((DEPTH, D_MODEL), 0.01)
    w_ple_gate = nrm((DEPTH, D_MODEL, D_MODEL), D_MODEL ** -0.5)
    b_ple_gate = nrm((DEPTH, D_MODEL), 0.01)
    w_ple_proj = nrm((DEPTH, PLE_DIM, D_MODEL), PLE_DIM ** -0.5)
    g_final = 1.0 + nrm((D_MODEL,), 0.01)
    return {'x_prompt': x_prompt, 'x_sample': x_sample, 'p_prompt': p_prompt, 'p_sample': p_sample,
            'cache_k': cache_k, 'cache_v': cache_v, 'state_ssm_re': state_ssm_re,
            'state_ssm_im': state_ssm_im, 'page_table': page_table,
            'g_mix': g_mix, 'w_in': w_in, 'lambda_q1': lambda_q1, 'lambda_k1': lambda_k1,
            'lambda_q2': lambda_q2, 'lambda_k2': lambda_k2, 'g_sub': g_sub,
            'a_re': a_re, 'a_im': a_im, 'log_dt': log_dt, 'b_re': b_re, 'b_im': b_im,
            'c_re': c_re, 'c_im': c_im, 'd_skip': d_skip, 'w_glu': w_glu, 'b_glu': b_glu,
            'g_ssm_out': g_ssm_out, 'w_o': w_o, 'g_ffn': g_ffn, 'w_router': w_router,
            'b_router': b_router, 'w_gu': w_gu, 'b_gu': b_gu, 'w_dn': w_dn, 'b_dn': b_dn,
            'g_ple': g_ple, 'w_ple_gate': w_ple_gate, 'b_ple_gate': b_ple_gate,
            'w_ple_proj': w_ple_proj, 'g_final': g_final}


def reference(x_prompt, x_sample, p_prompt, p_sample, cache_k, cache_v, state_ssm_re,
              state_ssm_im, page_table, g_mix, w_in, lambda_q1, lambda_k1, lambda_q2,
              lambda_k2, g_sub, a_re, a_im, log_dt, b_re, b_im, c_re, c_im, d_skip,
              w_glu, b_glu, g_ssm_out, w_o, g_ffn, w_router, b_router, w_gu, b_gu,
              w_dn, b_dn, g_ple, w_ple_gate, b_ple_gate, w_ple_proj, g_final):
    f32 = jnp.float32
    dec_b, dec_s = x_sample.shape[:2]
    past_len = page_table.shape[1] * PAGE_SIZE
    pos_p = jnp.arange(x_prompt.shape[1])
    pos_s = past_len + jnp.arange(dec_s)
    hp, hs = x_prompt, x_sample
    kp_l, vp_l, srp_l, sip_l, ks_l, vs_l, srs_l, sis_l = [], [], [], [], [], [], [], []
    for i in range(DEPTH):
        lam_init = 0.8 - 0.6 * math.exp(-0.3 * i)
        lam = (jnp.exp(jnp.sum(lambda_q1[i].astype(f32) * lambda_k1[i].astype(f32)))
               - jnp.exp(jnp.sum(lambda_q2[i].astype(f32) * lambda_k2[i].astype(f32)))
               + lam_init)
        lw = {'g_mix': g_mix[i], 'w_in': w_in[i], 'g_sub': g_sub[i], 'lam_init': lam_init,
              's5': s5_discretize(a_re[i], a_im[i], log_dt[i], b_re[i], b_im[i]),
              'c_re': c_re[i], 'c_im': c_im[i], 'd_skip': d_skip[i],
              'w_glu': w_glu[i], 'b_glu': b_glu[i], 'g_ssm_out': g_ssm_out[i], 'w_o': w_o[i],
              'g_ffn': g_ffn[i], 'w_router': w_router[i], 'b_router': b_router[i],
              'w_gu': w_gu[i], 'b_gu': b_gu[i], 'w_dn': w_dn[i], 'b_dn': b_dn[i],
              'g_ple': g_ple[i], 'w_ple_gate': w_ple_gate[i], 'b_ple_gate': b_ple_gate[i],
              'w_ple_proj': w_ple_proj[i]}
        hp, kp, vp, srp, sip = layer(hp, p_prompt[i], pos_p,
                                     lambda q, k, v: diff_attn_prompt(q, k, v, lam),
                                     None, None, lw)
        k_past = cache_k[i, page_table].reshape(dec_b, past_len, 2 * N_HEADS, SUB_DIM)
        v_past = cache_v[i, page_table].reshape(dec_b, past_len, N_HEADS, HEAD_DIM)
        hs, kn, vn, srs, sis = layer(hs, p_sample[i], pos_s,
                                     lambda q, k, v: diff_attn_sample(q, k, v, k_past, v_past, lam),
                                     state_ssm_re[i], state_ssm_im[i], lw)
        kp_l.append(kp); vp_l.append(vp); srp_l.append(srp); sip_l.append(sip)
        ks_l.append(kn); vs_l.append(vn); srs_l.append(srs); sis_l.append(sis)
    y_prompt = rmsnorm(hp, g_final)
    y_sample = rmsnorm(hs, g_final)
    return (y_prompt, y_sample,
            jnp.stack(kp_l), jnp.stack(vp_l), jnp.stack(srp_l), jnp.stack(sip_l),
            jnp.stack(ks_l), jnp.stack(vs_l), jnp.stack(srs_l), jnp.stack(sis_l))
```

```python
import functools
import math

import jax
import jax.numpy as jnp
from jax import lax
from jax.experimental import pallas as pl
from jax.experimental.pallas import tpu as pltpu

F32 = jnp.float32
BF16 = jnp.bfloat16
I32 = jnp.int32

N_HEADS = 8
HEAD_DIM = 128
SUB_DIM = 64
ROT_DIM = 16
ROPE_THETA = 500000.0
SSM_GROUP = 16
SSM_STATE = 64
N_EXPERTS = 32
TOP_K = 4
SWIGLU_LIMIT = 7.0
SWIGLU_ALPHA = 1.702
PAGE_SIZE = 128
NORM_EPS = 1e-6
SUBLN_EPS = 1e-5
LANES = 128
NEG_BIG = -1e30
VMEM_LIMIT = 56 * 1024 * 1024
PAGES_PER_STEP = 8


def _tile(n, pref):
    t = min(n, pref)
    while n % t:
        t -= 1
    return t


def _params(sem, **kw):
    return pltpu.CompilerParams(dimension_semantics=sem, vmem_limit_bytes=VMEM_LIMIT, **kw)


def _rms(x, g, eps):
    return x * lax.rsqrt(jnp.mean(x * x, axis=-1, keepdims=True) + eps) * g


def _bdot(a, b):
    return jnp.dot(a.astype(BF16), b.astype(BF16), preferred_element_type=F32)


def _disc_kernel(are_ref, aim_ref, ldt_ref, bre_ref, bim_ref, abre_ref, abim_ref, bbre_ref, bbim_ref):
    a_re = are_ref[...]
    a_im = aim_ref[...]
    dt = jnp.exp(ldt_ref[...])
    mag = jnp.exp(a_re * dt)
    ab_re = mag * jnp.cos(a_im * dt)
    ab_im = mag * jnp.sin(a_im * dt)
    nr, ni = ab_re - 1.0, ab_im
    den = a_re * a_re + a_im * a_im
    f_re = (nr * a_re + ni * a_im) / den
    f_im = (ni * a_re - nr * a_im) / den
    abre_ref[...] = ab_re
    abim_ref[...] = ab_im
    b_re = bre_ref[...]
    b_im = bim_ref[...]
    bbre_ref[...] = f_re * b_re - f_im * b_im
    bbim_ref[...] = f_re * b_im + f_im * b_re


def _discretize(a_re, a_im, log_dt, b_re, b_im):
    G, P = a_re.shape
    H = b_re.shape[-1]
    n = G * P
    row = lambda x: x.reshape(1, n)
    ldt = jnp.broadcast_to(log_dt[:, None], (G, P))
    bt = lambda x: jnp.transpose(x, (2, 0, 1)).reshape(H, n)
    out = pl.pallas_call(
        _disc_kernel,
        out_shape=(jax.ShapeDtypeStruct((1, n), F32), jax.ShapeDtypeStruct((1, n), F32),
                   jax.ShapeDtypeStruct((H, n), F32), jax.ShapeDtypeStruct((H, n), F32)),
        name="s5_discretize",
    )(row(a_re), row(a_im), row(ldt), bt(b_re), bt(b_im))
    return out


def _inproj_kernel(x_ref, g_ref, w_ref, cos_ref, sa_ref, sb_ref,
                   q_ref, k_ref, kb_ref, v_ref, vb_ref, u_ref, a_scr):
    j = pl.program_id(1)

    @pl.when(j == 0)
    def _():
        a_scr[...] = _rms(x_ref[...], g_ref[...], NORM_EPS).astype(BF16)

    acc = jnp.dot(a_scr[...], w_ref[...], preferred_element_type=F32)
    width = acc.shape[1]

    def rope_block(c):
        blk = acc[:, c * LANES:(c + 1) * LANES]
        return (blk * cos_ref[...] + pltpu.roll(blk, LANES - ROT_DIM // 2, 1) * sa_ref[...]
                + pltpu.roll(blk, ROT_DIM // 2, 1) * sb_ref[...])

    @pl.when(j == 0)
    def _():
        scale = SUB_DIM ** -0.5
        for c in range(width // LANES):
            q_ref[:, c * LANES:(c + 1) * LANES] = (rope_block(c) * scale).astype(BF16)

    @pl.when(j == 1)
    def _():
        for c in range(width // LANES):
            r = rope_block(c)
            k_ref[:, c * LANES:(c + 1) * LANES] = r
            kb_ref[:, c * LANES:(c + 1) * LANES] = r.astype(BF16)

    @pl.when(j == 2)
    def _():
        v_ref[...] = acc
        vb_ref[...] = acc.astype(BF16)

    @pl.when(j == 3)
    def _():
        u_ref[...] = acc


def _rope_tables(pos):
    half = ROT_DIM // 2
    inv = ROPE_THETA ** (-(jnp.arange(0, ROT_DIM, 2, dtype=F32) / ROT_DIM))
    ang = pos.astype(F32)[:, None] * inv[None, :]
    cos, sin = jnp.cos(ang), jnp.sin(ang)
    n = pos.shape[0]
    pad = jnp.zeros((n, SUB_DIM - ROT_DIM), F32)
    zer = jnp.zeros((n, half), F32)
    cos64 = jnp.concatenate([cos, cos, jnp.ones((n, SUB_DIM - ROT_DIM), F32)], axis=1)
    sa64 = jnp.concatenate([-sin, zer, pad], axis=1)
    sb64 = jnp.concatenate([zer, sin, pad], axis=1)
    rep = LANES // SUB_DIM
    return tuple(jnp.tile(t, (1, rep)) for t in (cos64, sa64, sb64))


def _inproj(x, g_mix, w_in_bf, tabs, tm):
    T, D = x.shape
    W = w_in_bf.shape[1] // 4
    ntab = tabs[0].shape[0] // tm
    row = lambda i, j: (i, 0)
    tab_spec = pl.BlockSpec((tm, LANES), lambda i, j: (i % ntab, 0))
    out_spec = pl.BlockSpec((tm, W), row)
    return pl.pallas_call(
        _inproj_kernel,
        grid=(T // tm, 4),
        in_specs=[pl.BlockSpec((tm, D), row),
                  pl.BlockSpec((1, D), lambda i, j: (0, 0)),
                  pl.BlockSpec((D, W), lambda i, j: (0, j)),
                  tab_spec, tab_spec, tab_spec],
        out_specs=(out_spec,) * 6,
        out_shape=(jax.ShapeDtypeStruct((T, W), BF16), jax.ShapeDtypeStruct((T, W), F32),
                   jax.ShapeDtypeStruct((T, W), BF16), jax.ShapeDtypeStruct((T, W), F32),
                   jax.ShapeDtypeStruct((T, W), BF16), jax.ShapeDtypeStruct((T, W), F32)),
        scratch_shapes=[pltpu.VMEM((tm, D), BF16)],
        compiler_params=_params(("parallel", "arbitrary")),
        name="inproj",
    )(x, g_mix.reshape(1, D), w_in_bf, *tabs)


def _lambda_value(lq1, lk1, lq2, lk2, lam_init):
    return (jnp.exp(jnp.sum(lq1[...] * lk1[...], axis=-1, keepdims=True))
            - jnp.exp(jnp.sum(lq2[...] * lk2[...], axis=-1, keepdims=True)) + lam_init)


def _attn_kernel(lq1, lk1, lq2, lk2, gsub_ref, q_ref, k_ref, v_ref, o_ref,
                 m_scr, l_scr, acc_scr, *, tq, tk, lam_init):
    qi = pl.program_id(1)
    ki = pl.program_id(2)

    @pl.when(ki == 0)
    def _():
        m_scr[...] = jnp.full(m_scr.shape, NEG_BIG, F32)
        l_scr[...] = jnp.zeros(l_scr.shape, F32)
        acc_scr[...] = jnp.zeros(acc_scr.shape, F32)

    last_k = (qi * tq + tq - 1) // tk

    @pl.when(ki <= last_k)
    def _():
        row = lax.broadcasted_iota(I32, (2 * tq, tk), 0)
        row = jnp.where(row >= tq, row - tq, row) + qi * tq
        col = lax.broadcasted_iota(I32, (2 * tq, tk), 1) + ki * tk
        visible = row >= col
        lane = lax.broadcasted_iota(I32, (tq, LANES), 1)
        for h in range(N_HEADS):
            sl = slice(h * HEAD_DIM, (h + 1) * HEAD_DIM)
            qh = q_ref[:, sl]
            zero = jnp.zeros_like(qh)
            q2 = jnp.concatenate([jnp.where(lane < SUB_DIM, qh, zero),
                                  jnp.where(lane >= SUB_DIM, qh, zero)], axis=0)
            s = lax.dot_general(q2, k_ref[:, sl], (((1,), (1,)), ((), ())),
                                preferred_element_type=F32)
            s = jnp.where(visible, s, NEG_BIG)
            m_old = m_scr[h]
            m_new = jnp.maximum(m_old, jnp.max(s, axis=1, keepdims=True))
            p = jnp.exp(s - m_new)
            alpha = jnp.exp(m_old - m_new)
            l_scr[h] = alpha * l_scr[h] + jnp.sum(p, axis=1, keepdims=True)
            acc_scr[h] = alpha * acc_scr[h] + jnp.dot(p.astype(BF16), v_ref[:, sl],
                                                      preferred_element_type=F32)
            m_scr[h] = m_new

    @pl.when(ki == pl.num_programs(2) - 1)
    def _():
        lam = _lambda_value(lq1, lk1, lq2, lk2, lam_init)
        for h in range(N_HEADS):
            a = acc_scr[h] / l_scr[h]
            o = a[:tq] - lam * a[tq:]
            o = _rms(o, gsub_ref[...], SUBLN_EPS) * (1.0 - lam_init)
            o_ref[:, h * HEAD_DIM:(h + 1) * HEAD_DIM] = o.astype(BF16)


def _attn_prompt(qb, kb, vb, lams, g_sub, nb, S, lam_init, tq, tk):
    T, W = qb.shape
    nq, nk = S // tq, S // tk
    vec = pl.BlockSpec((1, SUB_DIM), lambda b, i, j: (0, 0))

    def kv_map(b, i, j):
        return (b * nk + jnp.minimum(j, (i * tq + tq - 1) // tk), 0)

    return pl.pallas_call(
        functools.partial(_attn_kernel, tq=tq, tk=tk, lam_init=lam_init),
        grid=(nb, nq, nk),
        in_specs=[vec, vec, vec, vec,
                  pl.BlockSpec((1, HEAD_DIM), lambda b, i, j: (0, 0)),
                  pl.BlockSpec((tq, W), lambda b, i, j: (b * nq + i, 0)),
                  pl.BlockSpec((tk, W), kv_map),
                  pl.BlockSpec((tk, W), kv_map)],
        out_specs=pl.BlockSpec((tq, W), lambda b, i, j: (b * nq + i, 0)),
        out_shape=jax.ShapeDtypeStruct((T, W), BF16),
        scratch_shapes=[pltpu.VMEM((N_HEADS, 2 * tq, 1), F32),
                        pltpu.VMEM((N_HEADS, 2 * tq, 1), F32),
                        pltpu.VMEM((N_HEADS, 2 * tq, HEAD_DIM), F32)],
        compiler_params=_params(("parallel", "parallel", "arbitrary")),
        name="attn_prompt",
    )(*lams, g_sub.reshape(1, HEAD_DIM), qb, kb, vb)


def _attn_sample_kernel(pt_ref, lq1, lk1, lq2, lk2, gsub_ref, q_ref, kn_ref, vn_ref, *rest, lam_init):
    npg = PAGES_PER_STEP
    k_refs = rest[:npg]
    v_refs = rest[npg:2 * npg]
    o_ref, m_scr, l_scr, acc_scr = rest[2 * npg:]
    j = pl.program_id(1)
    W = q_ref.shape[-1]
    nrow = 2 * N_HEADS

    @pl.when(j == 0)
    def _():
        m_scr[...] = jnp.full(m_scr.shape, NEG_BIG, F32)
        l_scr[...] = jnp.zeros(l_scr.shape, F32)
        acc_scr[...] = jnp.zeros(acc_scr.shape, F32)

    rowi = lax.broadcasted_iota(I32, (nrow, W), 0)
    lanei = lax.broadcasted_iota(I32, (nrow, W), 1)
    sub_of_row = jnp.where(rowi < N_HEADS, 2 * rowi, 2 * (rowi - N_HEADS) + 1)
    own = (lanei // SUB_DIM) == sub_of_row
    qrow = q_ref[0].astype(F32)
    q_bd = jnp.where(own, jnp.broadcast_to(qrow, (nrow, W)), 0.0).astype(BF16)

    s = jnp.concatenate(
        [lax.dot_general(q_bd, k_refs[i][...].astype(BF16), (((1,), (1,)), ((), ())),
                         preferred_element_type=F32) for i in range(npg)], axis=1)
    m_old = m_scr[...]
    m_new = jnp.maximum(m_old, jnp.max(s, axis=1, keepdims=True))
    p = jnp.exp(s - m_new)
    alpha = jnp.exp(m_old - m_new)
    l_scr[...] = alpha * l_scr[...] + jnp.sum(p, axis=1, keepdims=True)
    pb = p.astype(BF16)
    pv = jnp.dot(pb[:, :PAGE_SIZE], v_refs[0][...].astype(BF16), preferred_element_type=F32)
    for i in range(1, npg):
        pv = pv + jnp.dot(pb[:, i * PAGE_SIZE:(i + 1) * PAGE_SIZE], v_refs[i][...].astype(BF16),
                          preferred_element_type=F32)
    acc_scr[...] = alpha * acc_scr[...] + pv
    m_scr[...] = m_new

    @pl.when(j == pl.num_programs(1) - 1)
    def _():
        kn = kn_ref[0].astype(BF16).astype(F32)
        vn = vn_ref[0].astype(BF16).astype(F32)
        s_new = jnp.sum(q_bd.astype(F32) * kn, axis=1, keepdims=True)
        m_o = m_scr[...]
        m_n = jnp.maximum(m_o, s_new)
        a = jnp.exp(m_o - m_n)
        p_n = jnp.exp(s_new - m_n)
        l = a * l_scr[...] + p_n
        acc = a * acc_scr[...] + p_n.astype(BF16).astype(F32) * vn
        o16 = acc / l
        lam = _lambda_value(lq1, lk1, lq2, lk2, lam_init)
        o8 = o16[:N_HEADS] - lam * o16[N_HEADS:]
        r8 = lax.broadcasted_iota(I32, (N_HEADS, W), 0)
        l8 = lax.broadcasted_iota(I32, (N_HEADS, W), 1)
        om = jnp.where((l8 // HEAD_DIM) == r8, o8, 0.0)
        ms = jnp.sum(om * om, axis=1, keepdims=True) / HEAD_DIM
        om = om * lax.rsqrt(ms + SUBLN_EPS)
        g = jnp.concatenate([gsub_ref[...]] * N_HEADS, axis=1)
        o_ref[0] = (jnp.sum(om, axis=0, keepdims=True) * g * (1.0 - lam_init)).astype(BF16)


def _attn_sample(qb, k_new, v_new, cache_k, cache_v, page_table, lams, g_sub, lam_init):
    nb, W = qb.shape
    n_pool = cache_k.shape[0]
    n_pages = page_table.shape[1]
    npg = PAGES_PER_STEP
    assert n_pages % npg == 0
    ck = cache_k.reshape(n_pool, PAGE_SIZE, W)
    cv = cache_v.reshape(n_pool, PAGE_SIZE, W)
    vec = pl.BlockSpec((1, SUB_DIM), lambda b, j, pt: (0, 0))
    rowspec = pl.BlockSpec((1, 1, W), lambda b, j, pt: (b, 0, 0))

    def page_spec(i):
        return pl.BlockSpec((None, PAGE_SIZE, W), lambda b, j, pt: (pt[b, j * npg + i], 0, 0))

    grid_spec = pltpu.PrefetchScalarGridSpec(
        num_scalar_prefetch=1,
        grid=(nb, n_pages // npg),
        in_specs=[vec, vec, vec, vec,
                  pl.BlockSpec((1, HEAD_DIM), lambda b, j, pt: (0, 0)),
                  rowspec, rowspec, rowspec]
                 + [page_spec(i) for i in range(npg)] + [page_spec(i) for i in range(npg)],
        out_specs=rowspec,
        scratch_shapes=[pltpu.VMEM((2 * N_HEADS, 1), F32), pltpu.VMEM((2 * N_HEADS, 1), F32),
                        pltpu.VMEM((2 * N_HEADS, W), F32)],
    )
    out = pl.pallas_call(
        functools.partial(_attn_sample_kernel, lam_init=lam_init),
        grid_spec=grid_spec,
        out_shape=jax.ShapeDtypeStruct((nb, 1, W), BF16),
        compiler_params=_params(("parallel", "arbitrary")),
        name="attn_sample",
    )(page_table, *lams, g_sub.reshape(1, HEAD_DIM), qb.reshape(nb, 1, W),
      k_new.reshape(nb, 1, W), v_new.reshape(nb, 1, W), *([ck] * npg), *([cv] * npg))
    return out.reshape(nb, W)


def _s5_kernel(u_ref, h0re_ref, h0im_ref, are_ref, aim_ref, bc_ref, cre_ref, cim_ref, d_ref,
               wglu_ref, bglu_ref, gout_ref,
               m_ref, stre_ref, stim_ref,
               utm_scr, xre_scr, xim_scr, sre_scr, sim_scr, *, nb, L, slab):
    c = pl.program_id(0)
    W = utm_scr.shape[1]
    NS = xre_scr.shape[1]
    kt = bc_ref.shape[0]
    cw = W // kt
    sw = NS // kt

    @pl.when(c == 0)
    def _():
        sre_scr[...] = h0re_ref[...]
        sim_scr[...] = h0im_ref[...]

    for t in range(L):
        utm_scr[t * nb:(t + 1) * nb, :] = u_ref[:, t:t + 1, :].reshape(nb, W)

    for k in range(kt):
        bu = _bdot(utm_scr[:, k * cw:(k + 1) * cw], bc_ref[k])
        xre_scr[:, k * sw:(k + 1) * sw] = bu[:, :sw]
        xim_scr[:, k * sw:(k + 1) * sw] = bu[:, sw:]

    def slab_body(j, carry):
        off = pl.multiple_of(j * slab, slab)
        lanes = pl.ds(off, slab)
        a_re = jnp.broadcast_to(are_ref[:, lanes], (nb, slab))
        a_im = jnp.broadcast_to(aim_ref[:, lanes], (nb, slab))
        s_re = sre_scr[:, lanes]
        s_im = sim_scr[:, lanes]
        for t in range(L):
            rows = slice(t * nb, (t + 1) * nb)
            n_re = a_re * s_re - a_im * s_im + xre_scr[rows, lanes]
            n_im = a_re * s_im + a_im * s_re + xim_scr[rows, lanes]
            xre_scr[rows, lanes] = n_re
            xim_scr[rows, lanes] = n_im
            s_re, s_im = n_re, n_im
        sre_scr[:, lanes] = s_re
        sim_scr[:, lanes] = s_im
        return carry

    lax.fori_loop(0, NS // slab, slab_body, 0)

    ys = []
    for k in range(kt):
        ys.append(_bdot(xre_scr[:, k * sw:(k + 1) * sw], cre_ref[k])
                  - _bdot(xim_scr[:, k * sw:(k + 1) * sw], cim_ref[k]))
    y = jnp.concatenate(ys, axis=1) + d_ref[...] * utm_scr[...]
    z = 0.5 * y * (1.0 + lax.erf(y * (2.0 ** -0.5)))
    gate = jax.nn.sigmoid(_bdot(z, wglu_ref[...]) + bglu_ref[...])
    m = _rms(z * gate, gout_ref[...], NORM_EPS)
    for t in range(L):
        m_ref[:, t:t + 1, :] = m[t * nb:(t + 1) * nb, :].reshape(nb, 1, W)

    @pl.when(c == pl.num_programs(0) - 1)
    def _():
        stre_ref[...] = sre_scr[...]
        stim_ref[...] = sim_scr[...]


def _s5(u3, h0_re, h0_im, consts, L):
    nb, S, W = u3.shape
    ab_re, ab_im, bc, cre, cim, d_row, wglu_bf, bglu, gout = consts
    NS = ab_re.shape[1]
    kt = bc.shape[0]
    slab = 512 if nb <= 16 else 256
    full2 = lambda a: pl.BlockSpec(a.shape, lambda c: (0, 0))
    full3 = lambda a: pl.BlockSpec(a.shape, lambda c: (0, 0, 0))
    rows = nb * L
    return pl.pallas_call(
        functools.partial(_s5_kernel, nb=nb, L=L, slab=slab),
        grid=(S // L,),
        in_specs=[pl.BlockSpec((nb, L, W), lambda c: (0, c, 0)),
                  full2(h0_re), full2(h0_im), full2(ab_re), full2(ab_im),
                  full3(bc), full3(cre), full3(cim), full2(d_row),
                  full2(wglu_bf), full2(bglu), full2(gout)],
        out_specs=(pl.BlockSpec((nb, L, W), lambda c: (0, c, 0)),
                   pl.BlockSpec((nb, NS), lambda c: (0, 0)),
                   pl.BlockSpec((nb, NS), lambda c: (0, 0))),
        out_shape=(jax.ShapeDtypeStruct((nb, S, W), F32),
                   jax.ShapeDtypeStruct((nb, NS), F32), jax.ShapeDtypeStruct((nb, NS), F32)),
        scratch_shapes=[pltpu.VMEM((rows, W), F32), pltpu.VMEM((rows, NS), F32),
                        pltpu.VMEM((rows, NS), F32), pltpu.VMEM((nb, NS), F32),
                        pltpu.VMEM((nb, NS), F32)],
        compiler_params=_params(("arbitrary",)),
        name="s5_scan_glu",
    )(u3, h0_re, h0_im, ab_re, ab_im, bc, cre, cim, d_row, wglu_bf, bglu, gout)


def _s5_constants(ab_re, ab_im, bb_re_t, bb_im_t, c_re, c_im, d_skip, w_glu, b_glu, g_ssm_out):
    G, H, P = c_re.shape
    gt = 256 // H
    kt = G // gt
    eye = jnp.eye(gt, dtype=F32)

    def b_tiles(bt):
        b = bt.reshape(H, kt, gt, P).transpose(1, 2, 0, 3)
        return (b[:, :, :, None, :] * eye[None, :, None, :, None]).reshape(kt, gt * H, gt * P)

    def c_tiles(cm):
        c = cm.reshape(kt, gt, H, P).transpose(0, 1, 3, 2)
        return (c[:, :, :, None, :] * eye[None, :, None, :, None]).reshape(kt, gt * P, gt * H)

    bc = jnp.concatenate([b_tiles(bb_re_t), b_tiles(bb_im_t)], axis=2).astype(BF16)
    W = G * H
    return (ab_re, ab_im, bc, c_tiles(c_re).astype(BF16), c_tiles(c_im).astype(BF16),
            d_skip.reshape(1, W), w_glu.astype(BF16), b_glu.reshape(1, W), g_ssm_out.reshape(1, W))


def _mix_router_kernel(x_ref, o_ref, m_ref, wo_ref, gffn_ref, wr_ref, br_ref,
                       h_ref, c_ref, idx_ref, wgt_ref, rank_ref, cnt_ref, cnt_scr, *, tm):
    i = pl.program_id(0)
    aw = o_ref.shape[1]

    @pl.when(i == 0)
    def _():
        cnt_scr[...] = jnp.zeros(cnt_scr.shape, F32)

    mix = (jnp.dot(o_ref[...], wo_ref[:aw, :], preferred_element_type=F32)
           + _bdot(m_ref[...], wo_ref[aw:, :]))
    h = x_ref[...] + mix
    h_ref[...] = h
    c = _rms(h, gffn_ref[...], NORM_EPS)
    c_ref[...] = c.reshape(c_ref.shape)

    logits = jnp.dot(c, wr_ref[...], preferred_element_type=F32,
                     precision=lax.Precision.HIGHEST) + br_ref[...]
    lane = lax.broadcasted_iota(I32, (tm, LANES), 1)
    lanef = lane.astype(F32)
    work = jnp.where(lane < N_EXPERTS, logits, -jnp.inf)
    idx_out = jnp.zeros((tm, LANES), F32)
    val_out = jnp.zeros((tm, LANES), F32)
    sel = jnp.zeros((tm, LANES), F32)
    picks = []
    for k in range(TOP_K):
        vmax = jnp.max(work, axis=1, keepdims=True)
        imax = jnp.min(jnp.where(work == vmax, lanef, float(LANES)), axis=1, keepdims=True)
        hit = lanef == imax
        picks.append(hit)
        sel = jnp.where(hit, 1.0, sel)
        work = jnp.where(hit, -jnp.inf, work)
        idx_out = jnp.where(lane == k, imax, idx_out)
        val_out = jnp.where(lane == k, vmax, val_out)
    v0 = val_out[:, 0:1]
    e = jnp.where(lane < TOP_K, jnp.exp(val_out - v0), 0.0)
    wgt_ref[...] = e / jnp.sum(e, axis=1, keepdims=True)
    idx_ref[...] = idx_out.astype(I32)
    r = lax.broadcasted_iota(I32, (tm, tm), 0)
    q = lax.broadcasted_iota(I32, (tm, tm), 1)
    tril = jnp.where(r > q, 1.0, 0.0).astype(BF16)
    before = jnp.dot(tril, sel.astype(BF16), preferred_element_type=F32) + cnt_scr[...]
    rank = jnp.zeros((tm, LANES), F32)
    for k in range(TOP_K):
        rk = jnp.sum(jnp.where(picks[k], before, 0.0), axis=1, keepdims=True)
        rank = jnp.where(lane == k, rk, rank)
    rank_ref[...] = rank.astype(I32)
    cnt_scr[...] = cnt_scr[...] + jnp.sum(sel, axis=0, keepdims=True)
    cnt_ref[...] = cnt_scr[...].astype(I32)


def _mix_router(x, o, m, wo_bf, g_ffn, wr_pad, br_pad, tm):
    T, D = x.shape
    aw = o.shape[1]
    row = lambda i: (i, 0)
    full = lambda a: pl.BlockSpec(a.shape, lambda i: (0,) * a.ndim)
    lane_out = pl.BlockSpec((tm, LANES), row)
    return pl.pallas_call(
        functools.partial(_mix_router_kernel, tm=tm),
        grid=(T // tm,),
        in_specs=[pl.BlockSpec((tm, D), row), pl.BlockSpec((tm, aw), row),
                  pl.BlockSpec((tm, D - aw), row), full(wo_bf),
                  pl.BlockSpec((1, D), lambda i: (0, 0)), full(wr_pad), full(br_pad)],
        out_specs=(pl.BlockSpec((tm, D), row), pl.BlockSpec((tm, 1, D), lambda i: (i, 0, 0)),
                   lane_out, lane_out, lane_out, pl.BlockSpec((1, LANES), lambda i: (0, 0))),
        out_shape=(jax.ShapeDtypeStruct((T, D), F32), jax.ShapeDtypeStruct((T, 1, D), F32),
                   jax.ShapeDtypeStruct((T, LANES), I32), jax.ShapeDtypeStruct((T, LANES), F32),
                   jax.ShapeDtypeStruct((T, LANES), I32), jax.ShapeDtypeStruct((1, LANES), I32)),
        scratch_shapes=[pltpu.VMEM((1, LANES), F32)],
        compiler_params=_params(("arbitrary",)),
        name="mix_router",
    )(x, o, m, wo_bf, g_ffn.reshape(1, D), wr_pad, br_pad)


def _moe_kernel(te_ref, tn_ref, src_ref, dst_ref, wrow_ref, c_hbm,
                wg_ref, wu_ref, bg_ref, bu_ref, wd_ref, bd_ref, slots_hbm,
                x_scr, xb_scr, acc_scr, gsem, ssem, *, plane, pad_rows):
    i = pl.program_id(0)
    f = pl.program_id(1)
    nf = pl.num_programs(1)
    nv = tn_ref[i]
    tile, D = acc_scr.shape

    @pl.when(jnp.logical_and(i == 0, f == 0))
    def _():
        x_scr[...] = jnp.zeros(x_scr.shape, F32)
        if pad_rows:
            for k in range(TOP_K):
                pltpu.make_async_copy(x_scr.at[pl.ds(0, pad_rows)],
                                      slots_hbm.at[pl.ds(k * plane + plane - pad_rows, pad_rows)],
                                      ssem).start()
            for k in range(TOP_K):
                pltpu.make_async_copy(x_scr.at[pl.ds(0, pad_rows)],
                                      slots_hbm.at[pl.ds(k * plane + plane - pad_rows, pad_rows)],
                                      ssem).wait()

    @pl.when(jnp.logical_and(f == 0, nv > 0))
    def _():
        def start(r, carry):
            pltpu.make_async_copy(c_hbm.at[src_ref[0, 0, r]], x_scr.at[r], gsem).start()
            return carry

        def wait(r, carry):
            pltpu.make_async_copy(c_hbm.at[0], x_scr.at[r], gsem).wait()
            return carry

        lax.fori_loop(0, nv, start, 0)
        lax.fori_loop(0, nv, wait, 0)
        acc_scr[...] = x_scr[...].reshape(tile, D)
        xb_scr[...] = acc_scr[...].astype(BF16)
        acc_scr[...] = jnp.zeros((tile, D), F32)

    @pl.when(nv > 0)
    def _():
        x = xb_scr[...]
        gate = jnp.dot(x, wg_ref[0], preferred_element_type=F32) + bg_ref[0]
        up = jnp.dot(x, wu_ref[0], preferred_element_type=F32) + bu_ref[0]
        gate = jnp.minimum(gate, SWIGLU_LIMIT)
        up = jnp.clip(up, -SWIGLU_LIMIT, SWIGLU_LIMIT)
        hmid = (up + 1.0) * gate * jax.nn.sigmoid(SWIGLU_ALPHA * gate)
        acc_scr[...] += jnp.dot(hmid.astype(BF16), wd_ref[0], preferred_element_type=F32)

    @pl.when(jnp.logical_and(f == nf - 1, nv > 0))
    def _():
        y = (acc_scr[...] + bd_ref[0]) * wrow_ref[...]
        x_scr[...] = y.reshape(tile, 1, D)

        def start(r, carry):
            pltpu.make_async_copy(x_scr.at[r], slots_hbm.at[dst_ref[0, 0, r]], ssem).start()
            return carry

        def wait(r, carry):
            pltpu.make_async_copy(x_scr.at[r], slots_hbm.at[0], ssem).wait()
            return carry

        lax.fori_loop(0, nv, start, 0)
        lax.fori_loop(0, nv, wait, 0)


def _moe(c3, tile_expert, tile_nvalid, src, dst, wrow, wgu_bf, b_gu, wdn_bf, b_dn, plane, tile, tf):
    R = src.shape[0]
    n_slots = TOP_K * plane
    pad_rows = plane - c3.shape[0]
    assert 0 <= pad_rows <= tile
    nt = R // tile
    E, D, F2 = wgu_bf.shape
    FF = F2 // 2
    nf = FF // tf

    def fblk(i, f, tn):
        return jnp.where(tn[i] > 0, f, nf - 1)

    grid_spec = pltpu.PrefetchScalarGridSpec(
        num_scalar_prefetch=2,
        grid=(nt, nf),
        in_specs=[
            pl.BlockSpec((1, 1, tile), lambda i, f, te, tn: (i, 0, 0), memory_space=pltpu.SMEM),
            pl.BlockSpec((1, 1, tile), lambda i, f, te, tn: (i, 0, 0), memory_space=pltpu.SMEM),
            pl.BlockSpec((tile, 1), lambda i, f, te, tn: (i, 0)),
            pl.BlockSpec(memory_space=pl.ANY),
            pl.BlockSpec((1, D, tf), lambda i, f, te, tn: (te[i], 0, fblk(i, f, tn))),
            pl.BlockSpec((1, D, tf), lambda i, f, te, tn: (te[i], 0, nf + fblk(i, f, tn))),
            pl.BlockSpec((1, 1, tf), lambda i, f, te, tn: (te[i], 0, fblk(i, f, tn))),
            pl.BlockSpec((1, 1, tf), lambda i, f, te, tn: (te[i], 0, nf + fblk(i, f, tn))),
            pl.BlockSpec((1, tf, D), lambda i, f, te, tn: (te[i], fblk(i, f, tn), 0)),
            pl.BlockSpec((1, 1, D), lambda i, f, te, tn: (te[i], 0, 0)),
        ],
        out_specs=pl.BlockSpec(memory_space=pl.ANY),
        scratch_shapes=[pltpu.VMEM((tile, 1, D), F32), pltpu.VMEM((tile, D), BF16),
                        pltpu.VMEM((tile, D), F32),
                        pltpu.SemaphoreType.DMA, pltpu.SemaphoreType.DMA],
    )
    return pl.pallas_call(
        functools.partial(_moe_kernel, plane=plane, pad_rows=pad_rows),
        grid_spec=grid_spec,
        out_shape=jax.ShapeDtypeStruct((n_slots, 1, D), F32),
        compiler_params=_params(("arbitrary", "arbitrary"), has_side_effects=True),
        name="moe_experts",
    )(tile_expert, tile_nvalid, src.reshape(nt, 1, tile), dst.reshape(nt, 1, tile),
      wrow.reshape(R, 1), c3, wgu_bf, wgu_bf, b_gu.reshape(E, 1, F2), b_gu.reshape(E, 1, F2),
      wdn_bf, b_dn.reshape(E, 1, D))


def _moe_plan(idx, wgt, rank, counts, tile, plane):
    T, K = idx.shape
    E = counts.shape[0]
    n_rows = T * K
    nt = -(-n_rows // tile) + E
    tiles_per = (counts + tile - 1) // tile
    tile_start = jnp.cumsum(tiles_per) - tiles_per
    pos = (tile_start[idx] * tile + rank).reshape(-1)
    R = nt * tile
    tok = jnp.repeat(jnp.arange(T, dtype=I32), K)
    slot = jnp.tile(jnp.arange(K, dtype=I32), T) * plane + tok
    src = jnp.zeros((R,), I32).at[pos].set(tok)
    dst = jnp.zeros((R,), I32).at[pos].set(slot)
    wrow = jnp.zeros((R,), F32).at[pos].set(wgt.reshape(-1))
    tid = jnp.arange(nt, dtype=I32)
    tile_end = tile_start + tiles_per
    te = jnp.sum((tid[:, None] >= tile_end[None, :]).astype(I32), axis=1)
    used = te < E
    te = jnp.minimum(te, E - 1)
    first = tile_start[te]
    nvalid = jnp.clip(counts[te] - (tid - first) * tile, 0, tile)
    nvalid = jnp.where(used, nvalid, 0).astype(I32)
    return te.astype(I32), nvalid, src, dst, wrow


def _ple_kernel(h_ref, s0, s1, s2, s3, p_ref, gple_ref, wg_ref, bg_ref, wp_ref, gfin_ref, y_ref, tmp_scr):
    tm, D = h_ref.shape
    h = h_ref[...]
    for s in (s0, s1, s2, s3):
        tmp_scr[...] = s[...].reshape(tm, D)
        h = h + tmp_scr[...]
    gate = jax.nn.sigmoid(_bdot(_rms(h, gple_ref[...], NORM_EPS), wg_ref[...]) + bg_ref[...])
    h = h + gate * _bdot(p_ref[...], wp_ref[...])
    y_ref[...] = _rms(h, gfin_ref[...], NORM_EPS)


def _ple(h, slots, slot_row0, n_tok_all, p, g_ple, wg_bf, b_g, wp_bf, g_final, tm):
    T, D = h.shape
    PD = p.shape[1]
    row = lambda i: (i, 0)
    full = lambda a: pl.BlockSpec(a.shape, lambda i: (0,) * a.ndim)
    base = slot_row0 // tm
    plane = n_tok_all // tm
    assert slot_row0 % tm == 0 and n_tok_all % tm == 0

    def slot_spec(k):
        return pl.BlockSpec((tm, 1, D), lambda i: (k * plane + base + i, 0, 0))

    vecD = pl.BlockSpec((1, D), lambda i: (0, 0))
    return pl.pallas_call(
        _ple_kernel,
        grid=(T // tm,),
        in_specs=[pl.BlockSpec((tm, D), row)] + [slot_spec(k) for k in range(TOP_K)]
                 + [pl.BlockSpec((tm, PD), row), vecD, full(wg_bf), vecD, full(wp_bf), vecD],
        out_specs=pl.BlockSpec((tm, D), row),
        out_shape=jax.ShapeDtypeStruct((T, D), F32),
        scratch_shapes=[pltpu.VMEM((tm, D), F32)],
        compiler_params=_params(("parallel",)),
        name="ple_final",
    )(h, slots, slots, slots, slots, p, g_ple.reshape(1, D), wg_bf, b_g.reshape(1, D), wp_bf,
      g_final.reshape(1, D))


def kernel(x_prompt, x_sample, p_prompt, p_sample, cache_k, cache_v, state_ssm_re, state_ssm_im,
           page_table, g_mix, w_in, lambda_q1, lambda_k1, lambda_q2, lambda_k2, g_sub, a_re, a_im,
           log_dt, b_re, b_im, c_re, c_im, d_skip, w_glu, b_glu, g_ssm_out, w_o, g_ffn, w_router,
           b_router, w_gu, b_gu, w_dn, b_dn, g_ple, w_ple_gate, b_ple_gate, w_ple_proj, g_final):
    B, S, D = x_prompt.shape
    DB, DS = x_sample.shape[:2]
    assert DS == 1 and w_in.shape[0] == 1
    Tp, Ts = B * S, DB * DS
    past_len = page_table.shape[1] * PAGE_SIZE
    lam_init = 0.8 - 0.6 * math.exp(-0.3 * 0)
    G, P = a_re.shape[1:]
    H = b_re.shape[-1]
    AW = N_HEADS * HEAD_DIM
    SW = G * H

    lams = tuple(v[0].reshape(1, SUB_DIM) for v in (lambda_q1, lambda_k1, lambda_q2, lambda_k2))
    w_in_bf = w_in[0].astype(BF16)
    wo_bf = w_o[0].astype(BF16)
    wgu_bf = w_gu[0].astype(BF16)
    wdn_bf = w_dn[0].astype(BF16)
    wpg_bf = w_ple_gate[0].astype(BF16)
    wpp_bf = w_ple_proj[0].astype(BF16)
    wr_pad = jnp.zeros((D, LANES), F32).at[:, :N_EXPERTS].set(w_router[0])
    br_pad = jnp.zeros((1, LANES), F32).at[:, :N_EXPERTS].set(b_router[0][None, :])

    ab_re, ab_im, bb_re_t, bb_im_t = _discretize(a_re[0], a_im[0], log_dt[0], b_re[0], b_im[0])
    s5c = _s5_constants(ab_re, ab_im, bb_re_t, bb_im_t, c_re[0], c_im[0], d_skip[0],
                        w_glu[0], b_glu[0], g_ssm_out[0])

    def front(x2, pos, tm):
        tabs = _rope_tables(pos)
        return _inproj(x2, g_mix[0], w_in_bf, tabs, tm)

    tm_p = _tile(S, 512)
    xp = x_prompt.reshape(Tp, D)
    qb, k_p, kb, v_p, vb, u_p = front(xp, jnp.arange(S), tm_p)
    ta = _tile(S, 512)
    o_p = _attn_prompt(qb, kb, vb, lams, g_sub[0], B, S, lam_init, ta, ta)
    zeros_state = jnp.zeros((B, G * P), F32)
    L = _tile(S, max(1, 256 // B))
    m_p, sre_p, sim_p = _s5(u_p.reshape(B, S, SW), zeros_state, zeros_state, s5c, L)
    tm_r = _tile(Tp, 256)
    h_p, c_p, idx_p, wgt_p, rank_p, cnt_p = _mix_router(
        xp, o_p, m_p.reshape(Tp, SW), wo_bf, g_ffn[0], wr_pad, br_pad, tm_r)

    xs = x_sample.reshape(Ts, D)
    qs, k_s, _, v_s, _, u_s = front(xs, jnp.full((Ts,), past_len, I32), Ts)
    o_s = _attn_sample(qs, k_s, v_s, cache_k[0], cache_v[0], page_table, lams, g_sub[0], lam_init)
    m_s, sre_s, sim_s = _s5(u_s.reshape(DB, 1, SW), state_ssm_re[0].reshape(DB, G * P),
                            state_ssm_im[0].reshape(DB, G * P), s5c, 1)
    h_s, c_s, idx_s, wgt_s, rank_s, cnt_s = _mix_router(
        xs, o_s, m_s.reshape(Ts, SW), wo_bf, g_ffn[0], wr_pad, br_pad, Ts)

    T_all = Tp + Ts
    cnt_p = cnt_p[0, :N_EXPERTS]
    counts = cnt_p + cnt_s[0, :N_EXPERTS]
    idx = jnp.concatenate([idx_p[:, :TOP_K], idx_s[:, :TOP_K]], axis=0)
    wgt = jnp.concatenate([wgt_p[:, :TOP_K], wgt_s[:, :TOP_K]], axis=0)
    rank = jnp.concatenate([rank_p[:, :TOP_K], rank_s[:, :TOP_K] + cnt_p[idx_s[:, :TOP_K]]], axis=0)
    c_all = jnp.concatenate([c_p, c_s], axis=0)
    tm_f = _tile(Tp, 256)
    tm_fs = Ts
    assert Tp % tm_fs == 0 and tm_f % tm_fs == 0
    plane = -(-T_all // tm_f) * tm_f
    tile = min(512, -(-T_all * TOP_K // 8) * 8)
    te, nvalid, src, dst, wrow = _moe_plan(idx, wgt, rank, counts, tile, plane)
    slots = _moe(c_all, te, nvalid, src, dst, wrow, wgu_bf, b_gu[0], wdn_bf, b_dn[0],
                 plane, tile, _tile(w_dn.shape[2], 512))

    y_p = _ple(h_p, slots, 0, plane, p_prompt[0].reshape(Tp, -1), g_ple[0], wpg_bf, b_ple_gate[0],
               wpp_bf, g_final, tm_f)
    y_s = _ple(h_s, slots, Tp, plane, p_sample[0].reshape(Ts, -1), g_ple[0], wpg_bf, b_ple_gate[0],
               wpp_bf, g_final, tm_fs)

    NK = 2 * N_HEADS
    return (y_p.reshape(B, S, D), y_s.reshape(DB, DS, D),
            k_p.reshape(1, B, S, NK, SUB_DIM), v_p.reshape(1, B, S, N_HEADS, HEAD_DIM),
            sre_p.reshape(1, B, G, P), sim_p.reshape(1, B, G, P),
            k_s.reshape(1, DB, DS, NK, SUB_DIM), v_s.reshape(1, DB, DS, N_HEADS, HEAD_DIM),
            sre_s.reshape(1, DB, G, P), sim_s.reshape(1, DB, G, P))
```

```python
import functools
import math

import jax
import jax.numpy as jnp
from jax import lax
from jax.experimental import pallas as pl
from jax.experimental.pallas import tpu as pltpu

F32 = jnp.float32
BF16 = jnp.bfloat16
I32 = jnp.int32

N_HEADS = 8
HEAD_DIM = 128
SUB_DIM = 64
ROT_DIM = 16
ROPE_THETA = 500000.0
SSM_GROUP = 16
SSM_STATE = 64
N_EXPERTS = 32
TOP_K = 4
SWIGLU_LIMIT = 7.0
SWIGLU_ALPHA = 1.702
PAGE_SIZE = 128
NORM_EPS = 1e-6
SUBLN_EPS = 1e-5
LANES = 128
NEG_BIG = -1e30
VMEM_LIMIT = 56 * 1024 * 1024
PAGES_PER_STEP = 8


def _tile(n, pref):
    t = min(n, pref)
    while n % t:
        t -= 1
    return t


def _params(sem, **kw):
    return pltpu.CompilerParams(dimension_semantics=sem, vmem_limit_bytes=VMEM_LIMIT, **kw)


def _rms(x, g, eps):
    return x * lax.rsqrt(jnp.mean(x * x, axis=-1, keepdims=True) + eps) * g


def _bdot(a, b):
    return jnp.dot(a.astype(BF16), b.astype(BF16), preferred_element_type=F32)


def _disc_kernel(are_ref, aim_ref, ldt_ref, bre_ref, bim_ref, abre_ref, abim_ref, bbre_ref, bbim_ref):
    a_re = are_ref[...]
    a_im = aim_ref[...]
    dt = jnp.exp(ldt_ref[...])
    mag = jnp.exp(a_re * dt)
    ab_re = mag * jnp.cos(a_im * dt)
    ab_im = mag * jnp.sin(a_im * dt)
    nr, ni = ab_re - 1.0, ab_im
    den = a_re * a_re + a_im * a_im
    f_re = (nr * a_re + ni * a_im) / den
    f_im = (ni * a_re - nr * a_im) / den
    abre_ref[...] = ab_re
    abim_ref[...] = ab_im
    b_re = bre_ref[...]
    b_im = bim_ref[...]
    bbre_ref[...] = f_re * b_re - f_im * b_im
    bbim_ref[...] = f_re * b_im + f_im * b_re


def _discretize(a_re, a_im, log_dt, b_re, b_im):
    G, P = a_re.shape
    H = b_re.shape[-1]
    n = G * P
    row = lambda x: x.reshape(1, n)
    ldt = jnp.broadcast_to(log_dt[:, None], (G, P))
    bt = lambda x: jnp.transpose(x, (2, 0, 1)).reshape(H, n)
    out = pl.pallas_call(
        _disc_kernel,
        out_shape=(jax.ShapeDtypeStruct((1, n), F32), jax.ShapeDtypeStruct((1, n), F32),
                   jax.ShapeDtypeStruct((H, n), F32), jax.ShapeDtypeStruct((H, n), F32)),
        name="s5_discretize",
    )(row(a_re), row(a_im), row(ldt), bt(b_re), bt(b_im))
    return out


def _inproj_kernel(x_ref, g_ref, w_ref, cos_ref, sa_ref, sb_ref,
                   q_ref, k_ref, kb_ref, v_ref, vb_ref, u_ref, a_scr):
    j = pl.program_id(1)

    @pl.when(j == 0)
    def _():
        a_scr[...] = _rms(x_ref[...], g_ref[...], NORM_EPS).astype(BF16)

    acc = jnp.dot(a_scr[...], w_ref[...], preferred_element_type=F32)
    width = acc.shape[1]

    def rope_block(c):
        blk = acc[:, c * LANES:(c + 1) * LANES]
        return (blk * cos_ref[...] + pltpu.roll(blk, LANES - ROT_DIM // 2, 1) * sa_ref[...]
                + pltpu.roll(blk, ROT_DIM // 2, 1) * sb_ref[...])

    @pl.when(j == 0)
    def _():
        scale = SUB_DIM ** -0.5
        for c in range(width // LANES):
            q_ref[:, c * LANES:(c + 1) * LANES] = (rope_block(c) * scale).astype(BF16)

    @pl.when(j == 1)
    def _():
        for c in range(width // LANES):
            r = rope_block(c)
            k_ref[:, c * LANES:(c + 1) * LANES] = r
            kb_ref[:, c * LANES:(c + 1) * LANES] = r.astype(BF16)

    @pl.when(j == 2)
    def _():
        v_ref[...] = acc
        vb_ref[...] = acc.astype(BF16)

    @pl.when(j == 3)
    def _():
        u_ref[...] = acc


def _rope_tables(pos):
    half = ROT_DIM // 2
    inv = ROPE_THETA ** (-(jnp.arange(0, ROT_DIM, 2, dtype=F32) / ROT_DIM))
    ang = pos.astype(F32)[:, None] * inv[None, :]
    cos, sin = jnp.cos(ang), jnp.sin(ang)
    n = pos.shape[0]
    pad = jnp.zeros((n, SUB_DIM - ROT_DIM), F32)
    zer = jnp.zeros((n, half), F32)
    cos64 = jnp.concatenate([cos, cos, jnp.ones((n, SUB_DIM - ROT_DIM), F32)], axis=1)
    sa64 = jnp.concatenate([-sin, zer, pad], axis=1)
    sb64 = jnp.concatenate([zer, sin, pad], axis=1)
    rep = LANES // SUB_DIM
    return tuple(jnp.tile(t, (1, rep)) for t in (cos64, sa64, sb64))


def _inproj(x, g_mix, w_in_bf, tabs, tm):
    T, D = x.shape
    W = w_in_bf.shape[1] // 4
    ntab = tabs[0].shape[0] // tm
    row = lambda i, j: (i, 0)
    tab_spec = pl.BlockSpec((tm, LANES), lambda i, j: (i % ntab, 0))
    out_spec = pl.BlockSpec((tm, W), row)
    return pl.pallas_call(
        _inproj_kernel,
        grid=(T // tm, 4),
        in_specs=[pl.BlockSpec((tm, D), row),
                  pl.BlockSpec((1, D), lambda i, j: (0, 0)),
                  pl.BlockSpec((D, W), lambda i, j: (0, j)),
                  tab_spec, tab_spec, tab_spec],
        out_specs=(out_spec,) * 6,
        out_shape=(jax.ShapeDtypeStruct((T, W), BF16), jax.ShapeDtypeStruct((T, W), F32),
                   jax.ShapeDtypeStruct((T, W), BF16), jax.ShapeDtypeStruct((T, W), F32),
                   jax.ShapeDtypeStruct((T, W), BF16), jax.ShapeDtypeStruct((T, W), F32)),
        scratch_shapes=[pltpu.VMEM((tm, D), BF16)],
        compiler_params=_params(("parallel", "arbitrary")),
        name="inproj",
    )(x, g_mix.reshape(1, D), w_in_bf, *tabs)


def _lambda_value(lq1, lk1, lq2, lk2, lam_init):
    return (jnp.exp(jnp.sum(lq1[...] * lk1[...], axis=-1, keepdims=True))
            - jnp.exp(jnp.sum(lq2[...] * lk2[...], axis=-1, keepdims=True)) + lam_init)


def _attn_kernel(lq1, lk1, lq2, lk2, gsub_ref, q_ref, k_ref, v_ref, o_ref,
                 m_scr, l_scr, acc_scr, *, tq, tk, lam_init):
    qi = pl.program_id(1)
    ki = pl.program_id(2)

    @pl.when(ki == 0)
    def _():
        m_scr[...] = jnp.full(m_scr.shape, NEG_BIG, F32)
        l_scr[...] = jnp.zeros(l_scr.shape, F32)
        acc_scr[...] = jnp.zeros(acc_scr.shape, F32)

    last_k = (qi * tq + tq - 1) // tk

    @pl.when(ki <= last_k)
    def _():
        row = lax.broadcasted_iota(I32, (2 * tq, tk), 0)
        row = jnp.where(row >= tq, row - tq, row) + qi * tq
        col = lax.broadcasted_iota(I32, (2 * tq, tk), 1) + ki * tk
        visible = row >= col
        lane = lax.broadcasted_iota(I32, (tq, LANES), 1)
        for h in range(N_HEADS):
            sl = slice(h * HEAD_DIM, (h + 1) * HEAD_DIM)
            qh = q_ref[:, sl]
            zero = jnp.zeros_like(qh)
            q2 = jnp.concatenate([jnp.where(lane < SUB_DIM, qh, zero),
                                  jnp.where(lane >= SUB_DIM, qh, zero)], axis=0)
            s = lax.dot_general(q2, k_ref[:, sl], (((1,), (1,)), ((), ())),
                                preferred_element_type=F32)
            s = jnp.where(visible, s, NEG_BIG)
            m_old = m_scr[h]
            m_new = jnp.maximum(m_old, jnp.max(s, axis=1, keepdims=True))
            p = jnp.exp(s - m_new)
            alpha = jnp.exp(m_old - m_new)
            l_scr[h] = alpha * l_scr[h] + jnp.sum(p, axis=1, keepdims=True)
            acc_scr[h] = alpha * acc_scr[h] + jnp.dot(p.astype(BF16), v_ref[:, sl],
                                                      preferred_element_type=F32)
            m_scr[h] = m_new

    @pl.when(ki == pl.num_programs(2) - 1)
    def _():
        lam = _lambda_value(lq1, lk1, lq2, lk2, lam_init)
        for h in range(N_HEADS):
            a = acc_scr[h] / l_scr[h]
            o = a[:tq] - lam * a[tq:]
            o = _rms(o, gsub_ref[...], SUBLN_EPS) * (1.0 - lam_init)
            o_ref[:, h * HEAD_DIM:(h + 1) * HEAD_DIM] = o.astype(BF16)


def _attn_prompt(qb, kb, vb, lams, g_sub, nb, S, lam_init, tq, tk):
    T, W = qb.shape
    nq, nk = S // tq, S // tk
    vec = pl.BlockSpec((1, SUB_DIM), lambda b, i, j: (0, 0))

    def kv_map(b, i, j):
        return (b * nk + jnp.minimum(j, (i * tq + tq - 1) // tk), 0)

    return pl.pallas_call(
        functools.partial(_attn_kernel, tq=tq, tk=tk, lam_init=lam_init),
        grid=(nb, nq, nk),
        in_specs=[vec, vec, vec, vec,
                  pl.BlockSpec((1, HEAD_DIM), lambda b, i, j: (0, 0)),
                  pl.BlockSpec((tq, W), lambda b, i, j: (b * nq + i, 0)),
                  pl.BlockSpec((tk, W), kv_map),
                  pl.BlockSpec((tk, W), kv_map)],
        out_specs=pl.BlockSpec((tq, W), lambda b, i, j: (b * nq + i, 0)),
        out_shape=jax.ShapeDtypeStruct((T, W), BF16),
        scratch_shapes=[pltpu.VMEM((N_HEADS, 2 * tq, 1), F32),
                        pltpu.VMEM((N_HEADS, 2 * tq, 1), F32),
                        pltpu.VMEM((N_HEADS, 2 * tq, HEAD_DIM), F32)],
        compiler_params=_params(("parallel", "parallel", "arbitrary")),
        name="attn_prompt",
    )(*lams, g_sub.reshape(1, HEAD_DIM), qb, kb, vb)


def _attn_sample_kernel(pt_ref, lq1, lk1, lq2, lk2, gsub_ref, q1_ref, q2_ref, kn1_ref, kn2_ref, vn_ref,
                        *rest, lam_init):
    npg = PAGES_PER_STEP
    k_refs = rest[:npg]
    v_refs = rest[npg:2 * npg]
    o_ref, m_scr, l_scr, acc_scr = rest[2 * npg:]
    j = pl.program_id(1)
    nh = N_HEADS
    rows_v = PAGE_SIZE * nh

    @pl.when(j == 0)
    def _():
        m_scr[...] = jnp.full(m_scr.shape, NEG_BIG, F32)
        l_scr[...] = jnp.zeros(l_scr.shape, F32)
        acc_scr[...] = jnp.zeros(acc_scr.shape, F32)

    q1 = q1_ref[0].astype(BF16)
    q2 = q2_ref[0].astype(BF16)
    contract = (((1,), (1,)), ((), ()))
    s1, s2 = [], []
    for i in range(npg):
        k_even = k_refs[i][pl.ds(0, rows_v, stride=2), :].astype(BF16)
        k_odd = k_refs[i][pl.ds(1, rows_v, stride=2), :].astype(BF16)
        s1.append(lax.dot_general(q1, k_even, contract, preferred_element_type=F32))
        s2.append(lax.dot_general(q2, k_odd, contract, preferred_element_type=F32))
    s = jnp.concatenate([jnp.concatenate(s1, axis=1), jnp.concatenate(s2, axis=1)], axis=0)
    rowi = lax.broadcasted_iota(I32, s.shape, 0)
    coli = lax.broadcasted_iota(I32, s.shape, 1)
    s = jnp.where((coli % nh) == (rowi % nh), s, NEG_BIG)
    m_old = m_scr[...]
    m_new = jnp.maximum(m_old, jnp.max(s, axis=1, keepdims=True))
    p = jnp.exp(s - m_new)
    alpha = jnp.exp(m_old - m_new)
    l_scr[...] = alpha * l_scr[...] + jnp.sum(p, axis=1, keepdims=True)
    pb = p.astype(BF16)
    pv = jnp.dot(pb[:, :rows_v], v_refs[0][...].astype(BF16), preferred_element_type=F32)
    for i in range(1, npg):
        pv = pv + jnp.dot(pb[:, i * rows_v:(i + 1) * rows_v], v_refs[i][...].astype(BF16),
                          preferred_element_type=F32)
    acc_scr[...] = alpha * acc_scr[...] + pv
    m_scr[...] = m_new

    @pl.when(j == pl.num_programs(1) - 1)
    def _():
        qs = jnp.concatenate([q1, q2], axis=0).astype(F32)
        kn = jnp.concatenate([kn1_ref[0], kn2_ref[0]], axis=0).astype(BF16).astype(F32)
        vn = vn_ref[0].astype(BF16).astype(F32)
        vn = jnp.concatenate([vn, vn], axis=0)
        s_new = jnp.sum(qs * kn, axis=1, keepdims=True)
        m_o = m_scr[...]
        m_n = jnp.maximum(m_o, s_new)
        a = jnp.exp(m_o - m_n)
        p_n = jnp.exp(s_new - m_n)
        l = a * l_scr[...] + p_n
        acc = a * acc_scr[...] + p_n.astype(BF16).astype(F32) * vn
        o16 = acc / l
        lam = _lambda_value(lq1, lk1, lq2, lk2, lam_init)
        o8 = o16[:nh] - lam * o16[nh:]
        o_ref[0] = _rms(o8, gsub_ref[...], SUBLN_EPS) * (1.0 - lam_init)


def _attn_sample(qb, k_new, v_new, cache_k, cache_v, page_table, lams, g_sub, lam_init):
    nb, W = qb.shape
    n_pool = cache_k.shape[0]
    n_pages = page_table.shape[1]
    npg = PAGES_PER_STEP
    nh = N_HEADS
    assert n_pages % npg == 0
    ck = cache_k.reshape(n_pool, PAGE_SIZE * 2 * nh, SUB_DIM)
    cv = cache_v.reshape(n_pool, PAGE_SIZE * nh, HEAD_DIM)
    q4 = qb.astype(F32).reshape(nb, nh, 2, SUB_DIM)
    k4 = k_new.reshape(nb, nh, 2, SUB_DIM)
    vec = pl.BlockSpec((1, SUB_DIM), lambda b, j, pt: (0, 0))
    sub_spec = pl.BlockSpec((1, nh, SUB_DIM), lambda b, j, pt: (b, 0, 0))
    head_spec = pl.BlockSpec((1, nh, HEAD_DIM), lambda b, j, pt: (b, 0, 0))

    def page_spec(i, rows, width):
        return pl.BlockSpec((None, rows, width), lambda b, j, pt: (pt[b, j * npg + i], 0, 0))

    grid_spec = pltpu.PrefetchScalarGridSpec(
        num_scalar_prefetch=1,
        grid=(nb, n_pages // npg),
        in_specs=[vec, vec, vec, vec,
                  pl.BlockSpec((1, HEAD_DIM), lambda b, j, pt: (0, 0)),
                  sub_spec, sub_spec, sub_spec, sub_spec, head_spec]
                 + [page_spec(i, PAGE_SIZE * 2 * nh, SUB_DIM) for i in range(npg)]
                 + [page_spec(i, PAGE_SIZE * nh, HEAD_DIM) for i in range(npg)],
        out_specs=head_spec,
        scratch_shapes=[pltpu.VMEM((2 * nh, 1), F32), pltpu.VMEM((2 * nh, 1), F32),
                        pltpu.VMEM((2 * nh, HEAD_DIM), F32)],
    )
    out = pl.pallas_call(
        functools.partial(_attn_sample_kernel, lam_init=lam_init),
        grid_spec=grid_spec,
        out_shape=jax.ShapeDtypeStruct((nb, nh, HEAD_DIM), F32),
        compiler_params=_params(("parallel", "arbitrary")),
        name="attn_sample",
    )(page_table, *lams, g_sub.reshape(1, HEAD_DIM), q4[:, :, 0], q4[:, :, 1], k4[:, :, 0], k4[:, :, 1],
      v_new.reshape(nb, nh, HEAD_DIM), *([ck] * npg), *([cv] * npg))
    return out.reshape(nb, W).astype(BF16)


def _s5_kernel(u_ref, h0re_ref, h0im_ref, are_ref, aim_ref, bc_ref, cre_ref, cim_ref, d_ref,
               wglu_ref, bglu_ref, gout_ref,
               m_ref, stre_ref, stim_ref,
               utm_scr, xre_scr, xim_scr, sre_scr, sim_scr, *, nb, L, slab):
    c = pl.program_id(0)
    W = utm_scr.shape[1]
    NS = xre_scr.shape[1]
    kt = bc_ref.shape[0]
    cw = W // kt
    sw = NS // kt

    @pl.when(c == 0)
    def _():
        sre_scr[...] = h0re_ref[...]
        sim_scr[...] = h0im_ref[...]

    for t in range(L):
        utm_scr[t * nb:(t + 1) * nb, :] = u_ref[:, t:t + 1, :].reshape(nb, W)

    for k in range(kt):
        bu = _bdot(utm_scr[:, k * cw:(k + 1) * cw], bc_ref[k])
        xre_scr[:, k * sw:(k + 1) * sw] = bu[:, :sw]
        xim_scr[:, k * sw:(k + 1) * sw] = bu[:, sw:]

    def slab_body(j, carry):
        off = pl.multiple_of(j * slab, slab)
        lanes = pl.ds(off, slab)
        a_re = jnp.broadcast_to(are_ref[:, lanes], (nb, slab))
        a_im = jnp.broadcast_to(aim_ref[:, lanes], (nb, slab))
        s_re = sre_scr[:, lanes]
        s_im = sim_scr[:, lanes]
        for t in range(L):
            rows = slice(t * nb, (t + 1) * nb)
            n_re = a_re * s_re - a_im * s_im + xre_scr[rows, lanes]
            n_im = a_re * s_im + a_im * s_re + xim_scr[rows, lanes]
            xre_scr[rows, lanes] = n_re
            xim_scr[rows, lanes] = n_im
            s_re, s_im = n_re, n_im
        sre_scr[:, lanes] = s_re
        sim_scr[:, lanes] = s_im
        return carry

    lax.fori_loop(0, NS // slab, slab_body, 0)

    ys = []
    for k in range(kt):
        ys.append(_bdot(xre_scr[:, k * sw:(k + 1) * sw], cre_ref[k])
                  - _bdot(xim_scr[:, k * sw:(k + 1) * sw], cim_ref[k]))
    y = jnp.concatenate(ys, axis=1) + d_ref[...] * utm_scr[...]
    z = 0.5 * y * (1.0 + lax.erf(y * (2.0 ** -0.5)))
    gate = jax.nn.sigmoid(_bdot(z, wglu_ref[...]) + bglu_ref[...])
    m = _rms(z * gate, gout_ref[...], NORM_EPS)
    for t in range(L):
        m_ref[:, t:t + 1, :] = m[t * nb:(t + 1) * nb, :].reshape(nb, 1, W)

    @pl.when(c == pl.num_programs(0) - 1)
    def _():
        stre_ref[...] = sre_scr[...]
        stim_ref[...] = sim_scr[...]


def _s5(u3, h0_re, h0_im, consts, L):
    nb, S, W = u3.shape
    ab_re, ab_im, bc, cre, cim, d_row, wglu_bf, bglu, gout = consts
    NS = ab_re.shape[1]
    kt = bc.shape[0]
    slab = 512 if nb <= 16 else 256
    full2 = lambda a: pl.BlockSpec(a.shape, lambda c: (0, 0))
    full3 = lambda a: pl.BlockSpec(a.shape, lambda c: (0, 0, 0))
    rows = nb * L
    return pl.pallas_call(
        functools.partial(_s5_kernel, nb=nb, L=L, slab=slab),
        grid=(S // L,),
        in_specs=[pl.BlockSpec((nb, L, W), lambda c: (0, c, 0)),
                  full2(h0_re), full2(h0_im), full2(ab_re), full2(ab_im),
                  full3(bc), full3(cre), full3(cim), full2(d_row),
                  full2(wglu_bf), full2(bglu), full2(gout)],
        out_specs=(pl.BlockSpec((nb, L, W), lambda c: (0, c, 0)),
                   pl.BlockSpec((nb, NS), lambda c: (0, 0)),
                   pl.BlockSpec((nb, NS), lambda c: (0, 0))),
        out_shape=(jax.ShapeDtypeStruct((nb, S, W), F32),
                   jax.ShapeDtypeStruct((nb, NS), F32), jax.ShapeDtypeStruct((nb, NS), F32)),
        scratch_shapes=[pltpu.VMEM((rows, W), F32), pltpu.VMEM((rows, NS), F32),
                        pltpu.VMEM((rows, NS), F32), pltpu.VMEM((nb, NS), F32),
                        pltpu.VMEM((nb, NS), F32)],
        compiler_params=_params(("arbitrary",)),
        name="s5_scan_glu",
    )(u3, h0_re, h0_im, ab_re, ab_im, bc, cre, cim, d_row, wglu_bf, bglu, gout)


def _s5_constants(ab_re, ab_im, bb_re_t, bb_im_t, c_re, c_im, d_skip, w_glu, b_glu, g_ssm_out):
    G, H, P = c_re.shape
    gt = 256 // H
    kt = G // gt
    eye = jnp.eye(gt, dtype=F32)

    def b_tiles(bt):
        b = bt.reshape(H, kt, gt, P).transpose(1, 2, 0, 3)
        return (b[:, :, :, None, :] * eye[None, :, None, :, None]).reshape(kt, gt * H, gt * P)

    def c_tiles(cm):
        c = cm.reshape(kt, gt, H, P).transpose(0, 1, 3, 2)
        return (c[:, :, :, None, :] * eye[None, :, None, :, None]).reshape(kt, gt * P, gt * H)

    bc = jnp.concatenate([b_tiles(bb_re_t), b_tiles(bb_im_t)], axis=2).astype(BF16)
    W = G * H
    return (ab_re, ab_im, bc, c_tiles(c_re).astype(BF16), c_tiles(c_im).astype(BF16),
            d_skip.reshape(1, W), w_glu.astype(BF16), b_glu.reshape(1, W), g_ssm_out.reshape(1, W))


def _mix_router_kernel(x_ref, o_ref, m_ref, wo_ref, gffn_ref, wr_ref, br_ref, cbuf_ref,
                       h_ref, c_ref, idx_ref, wgt_ref, rank_ref, cnt_ref, cnt_scr, *, tm):
    del cbuf_ref
    i = pl.program_id(0)
    aw = o_ref.shape[1]

    @pl.when(i == 0)
    def _():
        cnt_scr[...] = jnp.zeros(cnt_scr.shape, F32)

    mix = (jnp.dot(o_ref[...], wo_ref[:aw, :], preferred_element_type=F32)
           + _bdot(m_ref[...], wo_ref[aw:, :]))
    h = x_ref[...] + mix
    h_ref[...] = h
    c = _rms(h, gffn_ref[...], NORM_EPS)
    c_ref[...] = c.reshape(c_ref.shape)

    logits = jnp.dot(c, wr_ref[...], preferred_element_type=F32,
                     precision=lax.Precision.HIGHEST) + br_ref[...]
    lane = lax.broadcasted_iota(I32, (tm, LANES), 1)
    lanef = lane.astype(F32)
    work = jnp.where(lane < N_EXPERTS, logits, -jnp.inf)
    idx_out = jnp.zeros((tm, LANES), F32)
    val_out = jnp.zeros((tm, LANES), F32)
    sel = jnp.zeros((tm, LANES), F32)
    picks = []
    for k in range(TOP_K):
        vmax = jnp.max(work, axis=1, keepdims=True)
        imax = jnp.min(jnp.where(work == vmax, lanef, float(LANES)), axis=1, keepdims=True)
        hit = lanef == imax
        picks.append(hit)
        sel = jnp.where(hit, 1.0, sel)
        work = jnp.where(hit, -jnp.inf, work)
        idx_out = jnp.where(lane == k, imax, idx_out)
        val_out = jnp.where(lane == k, vmax, val_out)
    v0 = val_out[:, 0:1]
    e = jnp.where(lane < TOP_K, jnp.exp(val_out - v0), 0.0)
    wgt_ref[...] = e / jnp.sum(e, axis=1, keepdims=True)
    idx_ref[...] = idx_out.astype(I32)
    r = lax.broadcasted_iota(I32, (tm, tm), 0)
    q = lax.broadcasted_iota(I32, (tm, tm), 1)
    tril = jnp.where(r > q, 1.0, 0.0).astype(BF16)
    before = jnp.dot(tril, sel.astype(BF16), preferred_element_type=F32) + cnt_scr[...]
    rank = jnp.zeros((tm, LANES), F32)
    for k in range(TOP_K):
        rk = jnp.sum(jnp.where(picks[k], before, 0.0), axis=1, keepdims=True)
        rank = jnp.where(lane == k, rk, rank)
    rank_ref[...] = rank.astype(I32)
    cnt_scr[...] = cnt_scr[...] + jnp.sum(sel, axis=0, keepdims=True)
    cnt_ref[...] = cnt_scr[...].astype(I32)


def _mix_router(x, o, m, wo_bf, g_ffn, wr_pad, br_pad, tm, c_buf, c_row0):
    T, D = x.shape
    aw = o.shape[1]
    c_rows = c_buf.shape[0]
    assert c_row0 % tm == 0 and c_buf.shape == (c_rows, 1, D)
    row = lambda i: (i, 0)
    full = lambda a: pl.BlockSpec(a.shape, lambda i: (0,) * a.ndim)
    lane_out = pl.BlockSpec((tm, LANES), row)
    in_specs = [pl.BlockSpec((tm, D), row), pl.BlockSpec((tm, aw), row),
                pl.BlockSpec((tm, D - aw), row), full(wo_bf),
                pl.BlockSpec((1, D), lambda i: (0, 0)), full(wr_pad), full(br_pad),
                pl.BlockSpec(memory_space=pl.ANY)]
    args = [x, o, m, wo_bf, g_ffn.reshape(1, D), wr_pad, br_pad, c_buf]
    aliases = {7: 1}
    return pl.pallas_call(
        functools.partial(_mix_router_kernel, tm=tm),
        grid=(T // tm,),
        in_specs=in_specs,
        out_specs=(pl.BlockSpec((tm, D), row),
                   pl.BlockSpec((tm, 1, D), lambda i: (c_row0 // tm + i, 0, 0)),
                   lane_out, lane_out, lane_out, pl.BlockSpec((1, LANES), lambda i: (0, 0))),
        out_shape=(jax.ShapeDtypeStruct((T, D), F32), jax.ShapeDtypeStruct((c_rows, 1, D), F32),
                   jax.ShapeDtypeStruct((T, LANES), I32), jax.ShapeDtypeStruct((T, LANES), F32),
                   jax.ShapeDtypeStruct((T, LANES), I32), jax.ShapeDtypeStruct((1, LANES), I32)),
        scratch_shapes=[pltpu.VMEM((1, LANES), F32)],
        input_output_aliases=aliases,
        compiler_params=_params(("arbitrary",)),
        name="mix_router",
    )(*args)


def _moe_kernel(te_ref, tn_ref, info0_ref, infon_ref, infop_ref, c_hbm,
                wg_ref, wu_ref, bg_ref, bu_ref, wd_ref, bd_ref, slots_hbm,
                x_scr, y_scr, xb_scr, acc_scr, gsem, ssem, *, n_tok, plane):
    i = pl.program_id(0)
    f = pl.program_id(1)
    nf = pl.num_programs(1)
    tile, D = acc_scr.shape
    chunk = tile // nf
    n_rows = n_tok * TOP_K
    used = tn_ref[i] > 0
    prev_used = jnp.logical_and(i > 0, tn_ref[jnp.maximum(i - 1, 0)] > 0)
    slot = i % 2
    other = 1 - slot

    def gather_start(info_ref, r, buf):
        src = jnp.minimum(info_ref[0, 0, r] // TOP_K, n_tok - 1)
        pltpu.make_async_copy(c_hbm.at[src], x_scr.at[buf, r], gsem.at[buf]).start()

    def gather_wait(buf):
        for r in range(tile):
            pltpu.make_async_copy(c_hbm.at[0], x_scr.at[buf, r], gsem.at[buf]).wait()

    def scatter_start(info_ref, r, buf):
        v = info_ref[0, 0, r]
        dst = jnp.where(v < n_rows, (v % TOP_K) * plane + v // TOP_K, v + (TOP_K * plane - n_rows))
        pltpu.make_async_copy(y_scr.at[buf, r], slots_hbm.at[dst], ssem.at[buf]).start()

    def scatter_wait(buf):
        for r in range(tile):
            pltpu.make_async_copy(y_scr.at[buf, r], slots_hbm.at[0], ssem.at[buf]).wait()

    def tail_copy(k):
        return pltpu.make_async_copy(y_scr.at[0, pl.ds(0, plane - n_tok)],
                                     slots_hbm.at[pl.ds(k * plane + n_tok, plane - n_tok)], ssem.at[0])

    @pl.when(jnp.logical_and(i == 0, f == 0))
    def _():
        y_scr[...] = jnp.zeros(y_scr.shape, F32)
        if plane > n_tok:
            for k in range(TOP_K):
                tail_copy(k).start()
            for k in range(TOP_K):
                tail_copy(k).wait()

        def body(r, carry):
            gather_start(info0_ref, r, 0)
            return carry

        lax.fori_loop(0, tile, body, 0)

    @pl.when(jnp.logical_and(f == 0, jnp.logical_or(i == 0, prev_used)))
    def _():
        gather_wait(slot)

    @pl.when(jnp.logical_and(f == 0, prev_used))
    def _():
        scatter_wait(slot)

    @pl.when(jnp.logical_and(f == 0, used))
    def _():
        acc_scr[...] = x_scr[slot].reshape(tile, D)
        xb_scr[...] = acc_scr[...].astype(BF16)
        acc_scr[...] = jnp.zeros((tile, D), F32)

    @pl.when(used)
    def _():
        x = xb_scr[...]
        gate = jnp.dot(x, wg_ref[0], preferred_element_type=F32) + bg_ref[0]
        up = jnp.dot(x, wu_ref[0], preferred_element_type=F32) + bu_ref[0]
        gate = jnp.minimum(gate, SWIGLU_LIMIT)
        up = jnp.clip(up, -SWIGLU_LIMIT, SWIGLU_LIMIT)
        hmid = (up + 1.0) * gate * jax.nn.sigmoid(SWIGLU_ALPHA * gate)
        acc_scr[...] += jnp.dot(hmid.astype(BF16), wd_ref[0], preferred_element_type=F32)
        for q in range(chunk):
            r = f * chunk + q
            gather_start(infon_ref, r, other)
            scatter_start(infop_ref, r, other)

    @pl.when(jnp.logical_and(f == nf - 1, used))
    def _():
        y_scr[slot] = (acc_scr[...] + bd_ref[0]).reshape(tile, 1, D)

    @pl.when(jnp.logical_and(f == 0, jnp.logical_and(prev_used, jnp.logical_not(used))))
    def _():
        def body(r, carry):
            scatter_start(infop_ref, r, other)
            return carry

        lax.fori_loop(0, tile, body, 0)
        scatter_wait(other)


def _moe(c3, tile_expert, tile_nvalid, info, wgu_bf, b_gu, wdn_bf, b_dn, plane, tile, tf):
    n_tok = c3.shape[0]
    nt = info.shape[0] - 1
    E, D, F2 = wgu_bf.shape
    FF = F2 // 2
    nf = FF // tf
    assert tile % nf == 0 and 0 <= plane - n_tok <= tile

    def fblk(i, f, tn):
        return jnp.where(tn[i] > 0, f, nf - 1)

    smem_tile = lambda fn: pl.BlockSpec((1, 1, tile), fn, memory_space=pltpu.SMEM)
    grid_spec = pltpu.PrefetchScalarGridSpec(
        num_scalar_prefetch=2,
        grid=(nt, nf),
        in_specs=[
            smem_tile(lambda i, f, te, tn: (1, 0, 0)),
            smem_tile(lambda i, f, te, tn: (jnp.minimum(i + 2, nt), 0, 0)),
            smem_tile(lambda i, f, te, tn: (i, 0, 0)),
            pl.BlockSpec(memory_space=pl.ANY),
            pl.BlockSpec((1, D, tf), lambda i, f, te, tn: (te[i], 0, fblk(i, f, tn))),
            pl.BlockSpec((1, D, tf), lambda i, f, te, tn: (te[i], 0, nf + fblk(i, f, tn))),
            pl.BlockSpec((1, 1, tf), lambda i, f, te, tn: (te[i], 0, fblk(i, f, tn))),
            pl.BlockSpec((1, 1, tf), lambda i, f, te, tn: (te[i], 0, nf + fblk(i, f, tn))),
            pl.BlockSpec((1, tf, D), lambda i, f, te, tn: (te[i], fblk(i, f, tn), 0)),
            pl.BlockSpec((1, 1, D), lambda i, f, te, tn: (te[i], 0, 0)),
        ],
        out_specs=pl.BlockSpec(memory_space=pl.ANY),
        scratch_shapes=[pltpu.VMEM((2, tile, 1, D), F32), pltpu.VMEM((2, tile, 1, D), F32),
                        pltpu.VMEM((tile, D), BF16), pltpu.VMEM((tile, D), F32),
                        pltpu.SemaphoreType.DMA((2,)), pltpu.SemaphoreType.DMA((2,))],
    )
    return pl.pallas_call(
        functools.partial(_moe_kernel, n_tok=n_tok, plane=plane),
        grid_spec=grid_spec,
        out_shape=jax.ShapeDtypeStruct((TOP_K * plane + tile, 1, D), F32),
        compiler_params=_params(("arbitrary", "arbitrary")),
        name="moe_experts",
    )(tile_expert, tile_nvalid, info, info, info, c3, wgu_bf, wgu_bf,
      b_gu.reshape(E, 1, F2), b_gu.reshape(E, 1, F2), wdn_bf, b_dn.reshape(E, 1, D))


def _moe_plan(idx, rank, counts, tile):
    T, K = idx.shape
    E = counts.shape[0]
    n_rows = T * K
    nt = -(-n_rows // tile) + E + 1
    tiles_per = (counts + tile - 1) // tile
    tile_start = jnp.cumsum(tiles_per) - tiles_per
    pos = ((tile_start[idx] + 1) * tile + rank).reshape(-1)
    R = (nt + 1) * tile
    pad_info = n_rows + jnp.arange(R, dtype=I32) % tile
    info = pad_info.at[pos].set(jnp.arange(n_rows, dtype=I32))
    tid = jnp.arange(nt, dtype=I32)
    tile_end = tile_start + tiles_per
    te = jnp.sum((tid[:, None] >= tile_end[None, :]).astype(I32), axis=1)
    used = te < E
    te = jnp.minimum(te, E - 1)
    first = tile_start[te]
    nvalid = jnp.clip(counts[te] - (tid - first) * tile, 0, tile)
    nvalid = jnp.where(used, nvalid, 0).astype(I32)
    return te.astype(I32), nvalid, info.reshape(nt + 1, 1, tile)


def _ple_kernel(h_ref, s0, s1, s2, s3, w_ref, p_ref, gple_ref, wg_ref, bg_ref, wp_ref, gfin_ref,
                y_ref, tmp_scr):
    tm, D = h_ref.shape
    h = h_ref[...]
    w = w_ref[...]
    for k, s in enumerate((s0, s1, s2, s3)):
        tmp_scr[...] = s[...].reshape(tm, D)
        h = h + w[:, k:k + 1] * tmp_scr[...]
    gate = jax.nn.sigmoid(_bdot(_rms(h, gple_ref[...], NORM_EPS), wg_ref[...]) + bg_ref[...])
    h = h + gate * _bdot(p_ref[...], wp_ref[...])
    y_ref[...] = _rms(h, gfin_ref[...], NORM_EPS)


def _ple(h, slots, plane, wgt, tok0, p, g_ple, wg_bf, b_g, wp_bf, g_final, tm):
    T, D = h.shape
    PD = p.shape[1]
    row = lambda i: (i, 0)
    full = lambda a: pl.BlockSpec(a.shape, lambda i: (0,) * a.ndim)
    assert tok0 % tm == 0 and plane % tm == 0

    def slot_spec(k):
        return pl.BlockSpec((tm, 1, D), lambda i: ((k * plane + tok0) // tm + i, 0, 0))

    vecD = pl.BlockSpec((1, D), lambda i: (0, 0))
    return pl.pallas_call(
        _ple_kernel,
        grid=(T // tm,),
        in_specs=[pl.BlockSpec((tm, D), row)] + [slot_spec(k) for k in range(TOP_K)]
                 + [pl.BlockSpec((tm, LANES), row),
                    pl.BlockSpec((tm, PD), row), vecD, full(wg_bf), vecD, full(wp_bf), vecD],
        out_specs=pl.BlockSpec((tm, D), row),
        out_shape=jax.ShapeDtypeStruct((T, D), F32),
        scratch_shapes=[pltpu.VMEM((tm, D), F32)],
        compiler_params=_params(("parallel",)),
        name="ple_final",
    )(h, slots, slots, slots, slots, wgt, p, g_ple.reshape(1, D), wg_bf, b_g.reshape(1, D), wp_bf,
      g_final.reshape(1, D))


def kernel(x_prompt, x_sample, p_prompt, p_sample, cache_k, cache_v, state_ssm_re, state_ssm_im,
           page_table, g_mix, w_in, lambda_q1, lambda_k1, lambda_q2, lambda_k2, g_sub, a_re, a_im,
           log_dt, b_re, b_im, c_re, c_im, d_skip, w_glu, b_glu, g_ssm_out, w_o, g_ffn, w_router,
           b_router, w_gu, b_gu, w_dn, b_dn, g_ple, w_ple_gate, b_ple_gate, w_ple_proj, g_final):
    B, S, D = x_prompt.shape
    DB, DS = x_sample.shape[:2]
    assert DS == 1 and w_in.shape[0] == 1
    Tp, Ts = B * S, DB * DS
    past_len = page_table.shape[1] * PAGE_SIZE
    lam_init = 0.8 - 0.6 * math.exp(-0.3 * 0)
    G, P = a_re.shape[1:]
    H = b_re.shape[-1]
    AW = N_HEADS * HEAD_DIM
    SW = G * H

    lams = tuple(v[0].reshape(1, SUB_DIM) for v in (lambda_q1, lambda_k1, lambda_q2, lambda_k2))
    w_in_bf = w_in[0].astype(BF16)
    wo_bf = w_o[0].astype(BF16)
    wgu_bf = w_gu[0].astype(BF16)
    wdn_bf = w_dn[0].astype(BF16)
    wpg_bf = w_ple_gate[0].astype(BF16)
    wpp_bf = w_ple_proj[0].astype(BF16)
    wr_pad = jnp.zeros((D, LANES), F32).at[:, :N_EXPERTS].set(w_router[0])
    br_pad = jnp.zeros((1, LANES), F32).at[:, :N_EXPERTS].set(b_router[0][None, :])

    ab_re, ab_im, bb_re_t, bb_im_t = _discretize(a_re[0], a_im[0], log_dt[0], b_re[0], b_im[0])
    s5c = _s5_constants(ab_re, ab_im, bb_re_t, bb_im_t, c_re[0], c_im[0], d_skip[0],
                        w_glu[0], b_glu[0], g_ssm_out[0])

    def front(x2, pos, tm):
        tabs = _rope_tables(pos)
        return _inproj(x2, g_mix[0], w_in_bf, tabs, tm)

    tm_p = _tile(S, 512)
    xp = x_prompt.reshape(Tp, D)
    qb, k_p, kb, v_p, vb, u_p = front(xp, jnp.arange(S), tm_p)
    ta = _tile(S, 512)
    o_p = _attn_prompt(qb, kb, vb, lams, g_sub[0], B, S, lam_init, ta, ta)
    zeros_state = jnp.zeros((B, G * P), F32)
    L = _tile(S, max(1, 256 // B))
    m_p, sre_p, sim_p = _s5(u_p.reshape(B, S, SW), zeros_state, zeros_state, s5c, L)
    tm_r = _tile(Tp, 256)
    T_all = Tp + Ts
    assert Tp % Ts == 0
    c_all = jnp.zeros((T_all, 1, D), F32)
    h_p, c_all, idx_p, wgt_p, rank_p, cnt_p = _mix_router(
        xp, o_p, m_p.reshape(Tp, SW), wo_bf, g_ffn[0], wr_pad, br_pad, tm_r, c_all, 0)

    xs = x_sample.reshape(Ts, D)
    qs, k_s, _, v_s, _, u_s = front(xs, jnp.full((Ts,), past_len, I32), Ts)
    o_s = _attn_sample(qs, k_s, v_s, cache_k[0], cache_v[0], page_table, lams, g_sub[0], lam_init)
    m_s, sre_s, sim_s = _s5(u_s.reshape(DB, 1, SW), state_ssm_re[0].reshape(DB, G * P),
                            state_ssm_im[0].reshape(DB, G * P), s5c, 1)
    h_s, c_all, idx_s, wgt_s, rank_s, cnt_s = _mix_router(
        xs, o_s, m_s.reshape(Ts, SW), wo_bf, g_ffn[0], wr_pad, br_pad, Ts, c_all, Tp)

    cnt_p = cnt_p[0, :N_EXPERTS]
    counts = cnt_p + cnt_s[0, :N_EXPERTS]
    idx = jnp.concatenate([idx_p[:, :TOP_K], idx_s[:, :TOP_K]], axis=0)
    rank = jnp.concatenate([rank_p[:, :TOP_K], rank_s[:, :TOP_K] + cnt_p[idx_s[:, :TOP_K]]], axis=0)
    tile = 512
    tm_f = _tile(Tp, 256)
    assert tm_f % Ts == 0
    plane = -(-T_all // tm_f) * tm_f
    te, nvalid, info = _moe_plan(idx, rank, counts, tile)
    slots = _moe(c_all, te, nvalid, info, wgu_bf, b_gu[0], wdn_bf, b_dn[0], plane, tile,
                 _tile(w_dn.shape[2], 512))

    y_p = _ple(h_p, slots, plane, wgt_p, 0, p_prompt[0].reshape(Tp, -1), g_ple[0], wpg_bf,
               b_ple_gate[0], wpp_bf, g_final, tm_f)
    y_s = _ple(h_s, slots, plane, wgt_s, Tp, p_sample[0].reshape(Ts, -1), g_ple[0], wpg_bf,
               b_ple_gate[0], wpp_bf, g_final, Ts)

    NK = 2 * N_HEADS
    return (y_p.reshape(B, S, D), y_s.reshape(DB, DS, D),
            k_p.reshape(1, B, S, NK, SUB_DIM), v_p.reshape(1, B, S, N_HEADS, HEAD_DIM),
            sre_p.reshape(1, B, G, P), sim_p.reshape(1, B, G, P),
            k_s.reshape(1, DB, DS, NK, SUB_DIM), v_s.reshape(1, DB, DS, N_HEADS, HEAD_DIM),
            sre_s.reshape(1, DB, G, P), sim_s.reshape(1, DB, G, P))
```

```python
import functools
import math

import jax
import jax.numpy as jnp
from jax import lax
from jax.experimental import pallas as pl
from jax.experimental.pallas import tpu as pltpu

F32 = jnp.float32
BF16 = jnp.bfloat16
I32 = jnp.int32

N_HEADS = 8
HEAD_DIM = 128
SUB_DIM = 64
ROT_DIM = 16
ROPE_THETA = 500000.0
SSM_GROUP = 16
SSM_STATE = 64
N_EXPERTS = 32
TOP_K = 4
SWIGLU_LIMIT = 7.0
SWIGLU_ALPHA = 1.702
PAGE_SIZE = 128
NORM_EPS = 1e-6
SUBLN_EPS = 1e-5
LANES = 128
NEG_BIG = -1e30
VMEM_LIMIT = 56 * 1024 * 1024
PAGES_PER_STEP = 8


def _tile(n, pref):
    t = min(n, pref)
    while n % t:
        t -= 1
    return t


def _params(sem, **kw):
    return pltpu.CompilerParams(dimension_semantics=sem, vmem_limit_bytes=VMEM_LIMIT, **kw)


def _rms(x, g, eps):
    return x * lax.rsqrt(jnp.mean(x * x, axis=-1, keepdims=True) + eps) * g


def _bdot(a, b):
    return jnp.dot(a.astype(BF16), b.astype(BF16), preferred_element_type=F32)


def _disc_kernel(are_ref, aim_ref, ldt_ref, bre_ref, bim_ref, abre_ref, abim_ref, bbre_ref, bbim_ref):
    a_re = are_ref[...]
    a_im = aim_ref[...]
    dt = jnp.exp(ldt_ref[...])
    mag = jnp.exp(a_re * dt)
    ab_re = mag * jnp.cos(a_im * dt)
    ab_im = mag * jnp.sin(a_im * dt)
    nr, ni = ab_re - 1.0, ab_im
    den = a_re * a_re + a_im * a_im
    f_re = (nr * a_re + ni * a_im) / den
    f_im = (ni * a_re - nr * a_im) / den
    abre_ref[...] = ab_re
    abim_ref[...] = ab_im
    b_re = bre_ref[...]
    b_im = bim_ref[...]
    bbre_ref[...] = f_re * b_re - f_im * b_im
    bbim_ref[...] = f_re * b_im + f_im * b_re


def _discretize(a_re, a_im, log_dt, b_re, b_im):
    G, P = a_re.shape
    H = b_re.shape[-1]
    n = G * P
    row = lambda x: x.reshape(1, n)
    ldt = jnp.broadcast_to(log_dt[:, None], (G, P))
    bt = lambda x: jnp.transpose(x, (2, 0, 1)).reshape(H, n)
    out = pl.pallas_call(
        _disc_kernel,
        out_shape=(jax.ShapeDtypeStruct((1, n), F32), jax.ShapeDtypeStruct((1, n), F32),
                   jax.ShapeDtypeStruct((H, n), F32), jax.ShapeDtypeStruct((H, n), F32)),
        name="s5_discretize",
    )(row(a_re), row(a_im), row(ldt), bt(b_re), bt(b_im))
    return out


def _inproj_kernel(x_ref, g_ref, w_ref, cos_ref, sa_ref, sb_ref,
                   q_ref, k_ref, kb_ref, v_ref, vb_ref, u_ref, a_scr):
    j = pl.program_id(1)

    @pl.when(j == 0)
    def _():
        a_scr[...] = _rms(x_ref[...], g_ref[...], NORM_EPS).astype(BF16)

    acc = jnp.dot(a_scr[...], w_ref[...], preferred_element_type=F32)
    width = acc.shape[1]

    def rope_block(c):
        blk = acc[:, c * LANES:(c + 1) * LANES]
        return (blk * cos_ref[...] + pltpu.roll(blk, LANES - ROT_DIM // 2, 1) * sa_ref[...]
                + pltpu.roll(blk, ROT_DIM // 2, 1) * sb_ref[...])

    @pl.when(j == 0)
    def _():
        scale = SUB_DIM ** -0.5
        for c in range(width // LANES):
            q_ref[:, c * LANES:(c + 1) * LANES] = (rope_block(c) * scale).astype(BF16)

    @pl.when(j == 1)
    def _():
        for c in range(width // LANES):
            r = rope_block(c)
            k_ref[:, c * LANES:(c + 1) * LANES] = r
            kb_ref[:, c * LANES:(c + 1) * LANES] = r.astype(BF16)

    @pl.when(j == 2)
    def _():
        v_ref[...] = acc
        vb_ref[...] = acc.astype(BF16)

    @pl.when(j == 3)
    def _():
        u_ref[...] = acc


def _rope_tables(pos):
    half = ROT_DIM // 2
    inv = ROPE_THETA ** (-(jnp.arange(0, ROT_DIM, 2, dtype=F32) / ROT_DIM))
    ang = pos.astype(F32)[:, None] * inv[None, :]
    cos, sin = jnp.cos(ang), jnp.sin(ang)
    n = pos.shape[0]
    pad = jnp.zeros((n, SUB_DIM - ROT_DIM), F32)
    zer = jnp.zeros((n, half), F32)
    cos64 = jnp.concatenate([cos, cos, jnp.ones((n, SUB_DIM - ROT_DIM), F32)], axis=1)
    sa64 = jnp.concatenate([-sin, zer, pad], axis=1)
    sb64 = jnp.concatenate([zer, sin, pad], axis=1)
    rep = LANES // SUB_DIM
    return tuple(jnp.tile(t, (1, rep)) for t in (cos64, sa64, sb64))


def _inproj(x, g_mix, w_in_bf, tabs, tm):
    T, D = x.shape
    W = w_in_bf.shape[1] // 4
    ntab = tabs[0].shape[0] // tm
    row = lambda i, j: (i, 0)
    tab_spec = pl.BlockSpec((tm, LANES), lambda i, j: (i % ntab, 0))
    out_spec = pl.BlockSpec((tm, W), row)
    return pl.pallas_call(
        _inproj_kernel,
        grid=(T // tm, 4),
        in_specs=[pl.BlockSpec((tm, D), row),
                  pl.BlockSpec((1, D), lambda i, j: (0, 0)),
                  pl.BlockSpec((D, W), lambda i, j: (0, j)),
                  tab_spec, tab_spec, tab_spec],
        out_specs=(out_spec,) * 6,
        out_shape=(jax.ShapeDtypeStruct((T, W), BF16), jax.ShapeDtypeStruct((T, W), F32),
                   jax.ShapeDtypeStruct((T, W), BF16), jax.ShapeDtypeStruct((T, W), F32),
                   jax.ShapeDtypeStruct((T, W), BF16), jax.ShapeDtypeStruct((T, W), F32)),
        scratch_shapes=[pltpu.VMEM((tm, D), BF16)],
        compiler_params=_params(("parallel", "arbitrary")),
        name="inproj",
    )(x, g_mix.reshape(1, D), w_in_bf, *tabs)


def _lambda_value(lq1, lk1, lq2, lk2, lam_init):
    return (jnp.exp(jnp.sum(lq1[...] * lk1[...], axis=-1, keepdims=True))
            - jnp.exp(jnp.sum(lq2[...] * lk2[...], axis=-1, keepdims=True)) + lam_init)


def _attn_kernel(lq1, lk1, lq2, lk2, gsub_ref, q_ref, k_ref, v_ref, o_ref,
                 m_scr, l_scr, acc_scr, *, tq, tk, lam_init):
    qi = pl.program_id(1)
    ki = pl.program_id(2)

    @pl.when(ki == 0)
    def _():
        m_scr[...] = jnp.full(m_scr.shape, NEG_BIG, F32)
        l_scr[...] = jnp.zeros(l_scr.shape, F32)
        acc_scr[...] = jnp.zeros(acc_scr.shape, F32)

    last_k = (qi * tq + tq - 1) // tk

    @pl.when(ki <= last_k)
    def _():
        row = lax.broadcasted_iota(I32, (2 * tq, tk), 0)
        row = jnp.where(row >= tq, row - tq, row) + qi * tq
        col = lax.broadcasted_iota(I32, (2 * tq, tk), 1) + ki * tk
        visible = row >= col
        lane = lax.broadcasted_iota(I32, (tq, LANES), 1)
        for h in range(N_HEADS):
            sl = slice(h * HEAD_DIM, (h + 1) * HEAD_DIM)
            qh = q_ref[:, sl]
            zero = jnp.zeros_like(qh)
            q2 = jnp.concatenate([jnp.where(lane < SUB_DIM, qh, zero),
                                  jnp.where(lane >= SUB_DIM, qh, zero)], axis=0)
            s = lax.dot_general(q2, k_ref[:, sl], (((1,), (1,)), ((), ())),
                                preferred_element_type=F32)
            s = jnp.where(visible, s, NEG_BIG)
            m_old = m_scr[h]
            m_new = jnp.maximum(m_old, jnp.max(s, axis=1, keepdims=True))
            p = jnp.exp(s - m_new)
            alpha = jnp.exp(m_old - m_new)
            l_scr[h] = alpha * l_scr[h] + jnp.sum(p, axis=1, keepdims=True)
            acc_scr[h] = alpha * acc_scr[h] + jnp.dot(p.astype(BF16), v_ref[:, sl],
                                                      preferred_element_type=F32)
            m_scr[h] = m_new

    @pl.when(ki == pl.num_programs(2) - 1)
    def _():
        lam = _lambda_value(lq1, lk1, lq2, lk2, lam_init)
        for h in range(N_HEADS):
            a = acc_scr[h] / l_scr[h]
            o = a[:tq] - lam * a[tq:]
            o = _rms(o, gsub_ref[...], SUBLN_EPS) * (1.0 - lam_init)
            o_ref[:, h * HEAD_DIM:(h + 1) * HEAD_DIM] = o.astype(BF16)


def _attn_prompt(qb, kb, vb, lams, g_sub, nb, S, lam_init, tq, tk):
    T, W = qb.shape
    nq, nk = S // tq, S // tk
    vec = pl.BlockSpec((1, SUB_DIM), lambda b, i, j: (0, 0))

    def kv_map(b, i, j):
        return (b * nk + jnp.minimum(j, (i * tq + tq - 1) // tk), 0)

    return pl.pallas_call(
        functools.partial(_attn_kernel, tq=tq, tk=tk, lam_init=lam_init),
        grid=(nb, nq, nk),
        in_specs=[vec, vec, vec, vec,
                  pl.BlockSpec((1, HEAD_DIM), lambda b, i, j: (0, 0)),
                  pl.BlockSpec((tq, W), lambda b, i, j: (b * nq + i, 0)),
                  pl.BlockSpec((tk, W), kv_map),
                  pl.BlockSpec((tk, W), kv_map)],
        out_specs=pl.BlockSpec((tq, W), lambda b, i, j: (b * nq + i, 0)),
        out_shape=jax.ShapeDtypeStruct((T, W), BF16),
        scratch_shapes=[pltpu.VMEM((N_HEADS, 2 * tq, 1), F32),
                        pltpu.VMEM((N_HEADS, 2 * tq, 1), F32),
                        pltpu.VMEM((N_HEADS, 2 * tq, HEAD_DIM), F32)],
        compiler_params=_params(("parallel", "parallel", "arbitrary")),
        name="attn_prompt",
    )(*lams, g_sub.reshape(1, HEAD_DIM), qb, kb, vb)


def _attn_sample_kernel(pt_ref, lq1, lk1, lq2, lk2, gsub_ref, qbd_ref, q_ref, kn_ref, vn_ref,
                        *rest, lam_init):
    npg = PAGES_PER_STEP
    k_refs = rest[:npg]
    v_refs = rest[npg:2 * npg]
    o_ref, m_scr, l_scr, acc_scr, o_scr = rest[2 * npg:]
    j = pl.program_id(1)
    nh = N_HEADS
    ns = 2 * nh

    @pl.when(j == 0)
    def _():
        m_scr[...] = jnp.full(m_scr.shape, NEG_BIG, F32)
        l_scr[...] = jnp.zeros(l_scr.shape, F32)
        acc_scr[...] = jnp.zeros(acc_scr.shape, F32)

    qbd = qbd_ref[0].astype(BF16)
    s = jnp.concatenate([jnp.dot(qbd, k_refs[i][...].astype(BF16), preferred_element_type=F32)
                         for i in range(npg)], axis=1)
    m_old = m_scr[...]
    m_new = jnp.maximum(m_old, jnp.max(s, axis=1, keepdims=True))
    p = jnp.exp(s - m_new)
    alpha = jnp.exp(m_old - m_new)
    l_scr[...] = alpha * l_scr[...] + jnp.sum(p, axis=1, keepdims=True)
    pb = p.astype(BF16)
    row_head = lax.broadcasted_iota(I32, (ns, HEAD_DIM), 0) // 2
    pv = jnp.zeros((ns, HEAD_DIM), F32)
    for h in range(nh):
        v_h = jnp.concatenate([v_refs[i][pl.ds(h, PAGE_SIZE, stride=nh), :] for i in range(npg)],
                              axis=0).astype(BF16)
        pv = pv + jnp.where(row_head == h, jnp.dot(pb, v_h, preferred_element_type=F32), 0.0)
    acc_scr[...] = alpha * acc_scr[...] + pv
    m_scr[...] = m_new

    @pl.when(j == pl.num_programs(1) - 1)
    def _():
        qs = q_ref[0].astype(BF16).astype(F32)
        kn = kn_ref[0].astype(BF16).astype(F32)
        vn = vn_ref[0].astype(BF16).astype(F32)
        s_new = jnp.sum(qs * kn, axis=1, keepdims=True)
        m_o = m_scr[...]
        m_n = jnp.maximum(m_o, s_new)
        a = jnp.exp(m_o - m_n)
        p_n = jnp.exp(s_new - m_n)
        l = a * l_scr[...] + p_n
        acc = a * acc_scr[...] + p_n.astype(BF16).astype(F32) * vn
        o_scr[...] = acc / l
        lam = _lambda_value(lq1, lk1, lq2, lk2, lam_init)
        o8 = o_scr[pl.ds(0, nh, stride=2), :] - lam * o_scr[pl.ds(1, nh, stride=2), :]
        o_ref[0] = _rms(o8, gsub_ref[...], SUBLN_EPS) * (1.0 - lam_init)


def _attn_sample(qb, k_new, v_new, cache_k, cache_v, page_table, lams, g_sub, lam_init):
    nb, W = qb.shape
    n_pool = cache_k.shape[0]
    n_pages = page_table.shape[1]
    npg = PAGES_PER_STEP
    nh = N_HEADS
    assert n_pages % npg == 0
    ns = 2 * nh
    ck = jnp.transpose(cache_k, (0, 2, 3, 1)).reshape(n_pool, ns * SUB_DIM, PAGE_SIZE)
    cv = cache_v.reshape(n_pool, PAGE_SIZE * nh, HEAD_DIM)
    q16 = qb.astype(F32).reshape(nb, ns, SUB_DIM)
    qbd = (q16[:, :, None, :] * jnp.eye(ns, dtype=F32)[None, :, :, None]).reshape(nb, ns, ns * SUB_DIM)
    vn16 = jnp.repeat(v_new.reshape(nb, nh, HEAD_DIM), 2, axis=1)
    vec = pl.BlockSpec((1, SUB_DIM), lambda b, j, pt: (0, 0))
    per_seq = lambda a: pl.BlockSpec((1,) + a.shape[1:], lambda b, j, pt: (b, 0, 0))

    def page_spec(i, rows, width):
        return pl.BlockSpec((None, rows, width), lambda b, j, pt: (pt[b, j * npg + i], 0, 0))

    k16 = k_new.reshape(nb, ns, SUB_DIM)
    grid_spec = pltpu.PrefetchScalarGridSpec(
        num_scalar_prefetch=1,
        grid=(nb, n_pages // npg),
        in_specs=[vec, vec, vec, vec,
                  pl.BlockSpec((1, HEAD_DIM), lambda b, j, pt: (0, 0)),
                  per_seq(qbd), per_seq(q16), per_seq(k16), per_seq(vn16)]
                 + [page_spec(i, ns * SUB_DIM, PAGE_SIZE) for i in range(npg)]
                 + [page_spec(i, PAGE_SIZE * nh, HEAD_DIM) for i in range(npg)],
        out_specs=pl.BlockSpec((1, nh, HEAD_DIM), lambda b, j, pt: (b, 0, 0)),
        scratch_shapes=[pltpu.VMEM((ns, 1), F32), pltpu.VMEM((ns, 1), F32),
                        pltpu.VMEM((ns, HEAD_DIM), F32), pltpu.VMEM((ns, HEAD_DIM), F32)],
    )
    out = pl.pallas_call(
        functools.partial(_attn_sample_kernel, lam_init=lam_init),
        grid_spec=grid_spec,
        out_shape=jax.ShapeDtypeStruct((nb, nh, HEAD_DIM), F32),
        compiler_params=_params(("parallel", "arbitrary")),
        name="attn_sample",
    )(page_table, *lams, g_sub.reshape(1, HEAD_DIM), qbd, q16, k16, vn16, *([ck] * npg), *([cv] * npg))
    return out.reshape(nb, W).astype(BF16)


def _s5_kernel(u_ref, h0re_ref, h0im_ref, are_ref, aim_ref, bc_ref, cre_ref, cim_ref, d_ref,
               wglu_ref, bglu_ref, gout_ref,
               m_ref, stre_ref, stim_ref,
               utm_scr, xre_scr, xim_scr, sre_scr, sim_scr, *, nb, L, slab):
    c = pl.program_id(0)
    W = utm_scr.shape[1]
    NS = xre_scr.shape[1]
    kt = bc_ref.shape[0]
    cw = W // kt
    sw = NS // kt

    @pl.when(c == 0)
    def _():
        sre_scr[...] = h0re_ref[...]
        sim_scr[...] = h0im_ref[...]

    for t in range(L):
        utm_scr[t * nb:(t + 1) * nb, :] = u_ref[:, t:t + 1, :].reshape(nb, W)

    for k in range(kt):
        bu = _bdot(utm_scr[:, k * cw:(k + 1) * cw], bc_ref[k])
        xre_scr[:, k * sw:(k + 1) * sw] = bu[:, :sw]
        xim_scr[:, k * sw:(k + 1) * sw] = bu[:, sw:]

    def slab_body(j, carry):
        off = pl.multiple_of(j * slab, slab)
        lanes = pl.ds(off, slab)
        a_re = jnp.broadcast_to(are_ref[:, lanes], (nb, slab))
        a_im = jnp.broadcast_to(aim_ref[:, lanes], (nb, slab))
        s_re = sre_scr[:, lanes]
        s_im = sim_scr[:, lanes]
        for t in range(L):
            rows = slice(t * nb, (t + 1) * nb)
            n_re = a_re * s_re - a_im * s_im + xre_scr[rows, lanes]
            n_im = a_re * s_im + a_im * s_re + xim_scr[rows, lanes]
            xre_scr[rows, lanes] = n_re
            xim_scr[rows, lanes] = n_im
            s_re, s_im = n_re, n_im
        sre_scr[:, lanes] = s_re
        sim_scr[:, lanes] = s_im
        return carry

    lax.fori_loop(0, NS // slab, slab_body, 0)

    ys = []
    for k in range(kt):
        ys.append(_bdot(xre_scr[:, k * sw:(k + 1) * sw], cre_ref[k])
                  - _bdot(xim_scr[:, k * sw:(k + 1) * sw], cim_ref[k]))
    y = jnp.concatenate(ys, axis=1) + d_ref[...] * utm_scr[...]
    z = 0.5 * y * (1.0 + lax.erf(y * (2.0 ** -0.5)))
    gate = jax.nn.sigmoid(_bdot(z, wglu_ref[...]) + bglu_ref[...])
    m = _rms(z * gate, gout_ref[...], NORM_EPS)
    for t in range(L):
        m_ref[:, t:t + 1, :] = m[t * nb:(t + 1) * nb, :].reshape(nb, 1, W)

    @pl.when(c == pl.num_programs(0) - 1)
    def _():
        stre_ref[...] = sre_scr[...]
        stim_ref[...] = sim_scr[...]


def _s5(u3, h0_re, h0_im, consts, L):
    nb, S, W = u3.shape
    ab_re, ab_im, bc, cre, cim, d_row, wglu_bf, bglu, gout = consts
    NS = ab_re.shape[1]
    kt = bc.shape[0]
    slab = 512 if nb <= 16 else 256
    full2 = lambda a: pl.BlockSpec(a.shape, lambda c: (0, 0))
    full3 = lambda a: pl.BlockSpec(a.shape, lambda c: (0, 0, 0))
    rows = nb * L
    return pl.pallas_call(
        functools.partial(_s5_kernel, nb=nb, L=L, slab=slab),
        grid=(S // L,),
        in_specs=[pl.BlockSpec((nb, L, W), lambda c: (0, c, 0)),
                  full2(h0_re), full2(h0_im), full2(ab_re), full2(ab_im),
                  full3(bc), full3(cre), full3(cim), full2(d_row),
                  full2(wglu_bf), full2(bglu), full2(gout)],
        out_specs=(pl.BlockSpec((nb, L, W), lambda c: (0, c, 0)),
                   pl.BlockSpec((nb, NS), lambda c: (0, 0)),
                   pl.BlockSpec((nb, NS), lambda c: (0, 0))),
        out_shape=(jax.ShapeDtypeStruct((nb, S, W), F32),
                   jax.ShapeDtypeStruct((nb, NS), F32), jax.ShapeDtypeStruct((nb, NS), F32)),
        scratch_shapes=[pltpu.VMEM((rows, W), F32), pltpu.VMEM((rows, NS), F32),
                        pltpu.VMEM((rows, NS), F32), pltpu.VMEM((nb, NS), F32),
                        pltpu.VMEM((nb, NS), F32)],
        compiler_params=_params(("arbitrary",)),
        name="s5_scan_glu",
    )(u3, h0_re, h0_im, ab_re, ab_im, bc, cre, cim, d_row, wglu_bf, bglu, gout)


def _s5_constants(ab_re, ab_im, bb_re_t, bb_im_t, c_re, c_im, d_skip, w_glu, b_glu, g_ssm_out):
    G, H, P = c_re.shape
    gt = 256 // H
    kt = G // gt
    eye = jnp.eye(gt, dtype=F32)

    def b_tiles(bt):
        b = bt.reshape(H, kt, gt, P).transpose(1, 2, 0, 3)
        return (b[:, :, :, None, :] * eye[None, :, None, :, None]).reshape(kt, gt * H, gt * P)

    def c_tiles(cm):
        c = cm.reshape(kt, gt, H, P).transpose(0, 1, 3, 2)
        return (c[:, :, :, None, :] * eye[None, :, None, :, None]).reshape(kt, gt * P, gt * H)

    bc = jnp.concatenate([b_tiles(bb_re_t), b_tiles(bb_im_t)], axis=2).astype(BF16)
    W = G * H
    return (ab_re, ab_im, bc, c_tiles(c_re).astype(BF16), c_tiles(c_im).astype(BF16),
            d_skip.reshape(1, W), w_glu.astype(BF16), b_glu.reshape(1, W), g_ssm_out.reshape(1, W))


def _mix_router_kernel(x_ref, o_ref, m_ref, wo_ref, gffn_ref, wr_ref, br_ref, cbuf_ref,
                       h_ref, c_ref, idx_ref, wgt_ref, rank_ref, cnt_ref, cnt_scr, *, tm):
    del cbuf_ref
    i = pl.program_id(0)
    aw = o_ref.shape[1]

    @pl.when(i == 0)
    def _():
        cnt_scr[...] = jnp.zeros(cnt_scr.shape, F32)

    mix = (jnp.dot(o_ref[...], wo_ref[:aw, :], preferred_element_type=F32)
           + _bdot(m_ref[...], wo_ref[aw:, :]))
    h = x_ref[...] + mix
    h_ref[...] = h
    c = _rms(h, gffn_ref[...], NORM_EPS)
    c_ref[...] = c.reshape(c_ref.shape)

    logits = jnp.dot(c, wr_ref[...], preferred_element_type=F32,
                     precision=lax.Precision.HIGHEST) + br_ref[...]
    lane = lax.broadcasted_iota(I32, (tm, LANES), 1)
    lanef = lane.astype(F32)
    work = jnp.where(lane < N_EXPERTS, logits, -jnp.inf)
    idx_out = jnp.zeros((tm, LANES), F32)
    val_out = jnp.zeros((tm, LANES), F32)
    sel = jnp.zeros((tm, LANES), F32)
    picks = []
    for k in range(TOP_K):
        vmax = jnp.max(work, axis=1, keepdims=True)
        imax = jnp.min(jnp.where(work == vmax, lanef, float(LANES)), axis=1, keepdims=True)
        hit = lanef == imax
        picks.append(hit)
        sel = jnp.where(hit, 1.0, sel)
        work = jnp.where(hit, -jnp.inf, work)
        idx_out = jnp.where(lane == k, imax, idx_out)
        val_out = jnp.where(lane == k, vmax, val_out)
    v0 = val_out[:, 0:1]
    e = jnp.where(lane < TOP_K, jnp.exp(val_out - v0), 0.0)
    wgt_ref[...] = e / jnp.sum(e, axis=1, keepdims=True)
    idx_ref[...] = idx_out.astype(I32)
    r = lax.broadcasted_iota(I32, (tm, tm), 0)
    q = lax.broadcasted_iota(I32, (tm, tm), 1)
    tril = jnp.where(r > q, 1.0, 0.0).astype(BF16)
    before = jnp.dot(tril, sel.astype(BF16), preferred_element_type=F32) + cnt_scr[...]
    rank = jnp.zeros((tm, LANES), F32)
    for k in range(TOP_K):
        rk = jnp.sum(jnp.where(picks[k], before, 0.0), axis=1, keepdims=True)
        rank = jnp.where(lane == k, rk, rank)
    rank_ref[...] = rank.astype(I32)
    cnt_scr[...] = cnt_scr[...] + jnp.sum(sel, axis=0, keepdims=True)
    cnt_ref[...] = cnt_scr[...].astype(I32)


def _mix_router(x, o, m, wo_bf, g_ffn, wr_pad, br_pad, tm, c_buf, c_row0):
    T, D = x.shape
    aw = o.shape[1]
    c_rows = c_buf.shape[0]
    assert c_row0 % tm == 0 and c_buf.shape == (c_rows, 1, D)
    row = lambda i: (i, 0)
    full = lambda a: pl.BlockSpec(a.shape, lambda i: (0,) * a.ndim)
    lane_out = pl.BlockSpec((tm, LANES), row)
    in_specs = [pl.BlockSpec((tm, D), row), pl.BlockSpec((tm, aw), row),
                pl.BlockSpec((tm, D - aw), row), full(wo_bf),
                pl.BlockSpec((1, D), lambda i: (0, 0)), full(wr_pad), full(br_pad),
                pl.BlockSpec(memory_space=pl.ANY)]
    args = [x, o, m, wo_bf, g_ffn.reshape(1, D), wr_pad, br_pad, c_buf]
    aliases = {7: 1}
    return pl.pallas_call(
        functools.partial(_mix_router_kernel, tm=tm),
        grid=(T // tm,),
        in_specs=in_specs,
        out_specs=(pl.BlockSpec((tm, D), row),
                   pl.BlockSpec((tm, 1, D), lambda i: (c_row0 // tm + i, 0, 0)),
                   lane_out, lane_out, lane_out, pl.BlockSpec((1, LANES), lambda i: (0, 0))),
        out_shape=(jax.ShapeDtypeStruct((T, D), F32), jax.ShapeDtypeStruct((c_rows, 1, D), F32),
                   jax.ShapeDtypeStruct((T, LANES), I32), jax.ShapeDtypeStruct((T, LANES), F32),
                   jax.ShapeDtypeStruct((T, LANES), I32), jax.ShapeDtypeStruct((1, LANES), I32)),
        scratch_shapes=[pltpu.VMEM((1, LANES), F32)],
        input_output_aliases=aliases,
        compiler_params=_params(("arbitrary",)),
        name="mix_router",
    )(*args)


def _moe_kernel(te_ref, tn_ref, info0_ref, infon_ref, infop_ref, c_hbm,
                wg_ref, wu_ref, bg_ref, bu_ref, wd_ref, bd_ref, slots_hbm,
                x_scr, y_scr, xb_scr, acc_scr, gsem, ssem, *, n_tok, plane):
    i = pl.program_id(0)
    f = pl.program_id(1)
    nf = pl.num_programs(1)
    tile, D = acc_scr.shape
    chunk = tile // nf
    n_rows = n_tok * TOP_K
    used = tn_ref[i] > 0
    prev_used = jnp.logical_and(i > 0, tn_ref[jnp.maximum(i - 1, 0)] > 0)
    slot = i % 2
    other = 1 - slot

    def gather_start(info_ref, r, buf):
        src = jnp.minimum(info_ref[0, 0, r] // TOP_K, n_tok - 1)
        pltpu.make_async_copy(c_hbm.at[src], x_scr.at[buf, r], gsem.at[buf]).start()

    def gather_wait(buf):
        pltpu.make_async_copy(c_hbm.at[pl.ds(0, tile)], x_scr.at[buf], gsem.at[buf]).wait()

    def scatter_start(info_ref, r, buf):
        v = info_ref[0, 0, r]
        dst = jnp.where(v < n_rows, (v % TOP_K) * plane + v // TOP_K, v + (TOP_K * plane - n_rows))
        pltpu.make_async_copy(y_scr.at[buf, r], slots_hbm.at[dst], ssem.at[buf]).start()

    def scatter_wait(buf):
        pltpu.make_async_copy(y_scr.at[buf], slots_hbm.at[pl.ds(0, tile)], ssem.at[buf]).wait()

    def tail_copy(k):
        return pltpu.make_async_copy(y_scr.at[0, pl.ds(0, plane - n_tok)],
                                     slots_hbm.at[pl.ds(k * plane + n_tok, plane - n_tok)], ssem.at[0])

    @pl.when(jnp.logical_and(i == 0, f == 0))
    def _():
        y_scr[...] = jnp.zeros(y_scr.shape, F32)
        if plane > n_tok:
            for k in range(TOP_K):
                tail_copy(k).start()
            for k in range(TOP_K):
                tail_copy(k).wait()

        def body(r, carry):
            gather_start(info0_ref, r, 0)
            return carry

        lax.fori_loop(0, tile, body, 0)

    @pl.when(jnp.logical_and(f == 0, jnp.logical_or(i == 0, prev_used)))
    def _():
        gather_wait(slot)

    @pl.when(jnp.logical_and(f == 0, prev_used))
    def _():
        scatter_wait(slot)

    @pl.when(jnp.logical_and(f == 0, used))
    def _():
        xb_scr[...] = x_scr[slot].reshape(tile, D).astype(BF16)
        acc_scr[...] = jnp.zeros((tile, D), F32)

    @pl.when(used)
    def _():
        x = xb_scr[...]
        gate = jnp.dot(x, wg_ref[0], preferred_element_type=F32) + bg_ref[0]
        up = jnp.dot(x, wu_ref[0], preferred_element_type=F32) + bu_ref[0]
        gate = jnp.minimum(gate, SWIGLU_LIMIT)
        up = jnp.clip(up, -SWIGLU_LIMIT, SWIGLU_LIMIT)
        hmid = (up + 1.0) * gate * jax.nn.sigmoid(SWIGLU_ALPHA * gate)
        acc_scr[...] += jnp.dot(hmid.astype(BF16), wd_ref[0], preferred_element_type=F32)
        for q in range(chunk):
            r = f * chunk + q
            gather_start(infon_ref, r, other)
            scatter_start(infop_ref, r, other)

    @pl.when(jnp.logical_and(f == nf - 1, used))
    def _():
        y_scr[slot] = (acc_scr[...] + bd_ref[0]).reshape(tile, 1, D)

    @pl.when(jnp.logical_and(f == 0, jnp.logical_and(prev_used, jnp.logical_not(used))))
    def _():
        def body(r, carry):
            scatter_start(infop_ref, r, other)
            return carry

        lax.fori_loop(0, tile, body, 0)
        scatter_wait(other)


def _moe(c3, tile_expert, tile_nvalid, info, wgu_bf, b_gu, wdn_bf, b_dn, plane, tile, tf):
    n_tok = c3.shape[0]
    nt = info.shape[0] - 1
    E, D, F2 = wgu_bf.shape
    FF = F2 // 2
    nf = FF // tf
    assert tile % nf == 0 and 0 <= plane - n_tok <= tile

    def fblk(i, f, tn):
        return jnp.where(tn[i] > 0, f, nf - 1)

    smem_tile = lambda fn: pl.BlockSpec((1, 1, tile), fn, memory_space=pltpu.SMEM)
    grid_spec = pltpu.PrefetchScalarGridSpec(
        num_scalar_prefetch=2,
        grid=(nt, nf),
        in_specs=[
            smem_tile(lambda i, f, te, tn: (1, 0, 0)),
            smem_tile(lambda i, f, te, tn: (jnp.minimum(i + 2, nt), 0, 0)),
            smem_tile(lambda i, f, te, tn: (i, 0, 0)),
            pl.BlockSpec(memory_space=pl.ANY),
            pl.BlockSpec((1, D, tf), lambda i, f, te, tn: (te[i], 0, fblk(i, f, tn))),
            pl.BlockSpec((1, D, tf), lambda i, f, te, tn: (te[i], 0, nf + fblk(i, f, tn))),
            pl.BlockSpec((1, 1, tf), lambda i, f, te, tn: (te[i], 0, fblk(i, f, tn))),
            pl.BlockSpec((1, 1, tf), lambda i, f, te, tn: (te[i], 0, nf + fblk(i, f, tn))),
            pl.BlockSpec((1, tf, D), lambda i, f, te, tn: (te[i], fblk(i, f, tn), 0)),
            pl.BlockSpec((1, 1, D), lambda i, f, te, tn: (te[i], 0, 0)),
        ],
        out_specs=pl.BlockSpec(memory_space=pl.ANY),
        scratch_shapes=[pltpu.VMEM((2, tile, 1, D), F32), pltpu.VMEM((2, tile, 1, D), F32),
                        pltpu.VMEM((tile, D), BF16), pltpu.VMEM((tile, D), F32),
                        pltpu.SemaphoreType.DMA((2,)), pltpu.SemaphoreType.DMA((2,))],
    )
    return pl.pallas_call(
        functools.partial(_moe_kernel, n_tok=n_tok, plane=plane),
        grid_spec=grid_spec,
        out_shape=jax.ShapeDtypeStruct((TOP_K * plane + tile, 1, D), F32),
        compiler_params=_params(("arbitrary", "arbitrary")),
        name="moe_experts",
    )(tile_expert, tile_nvalid, info, info, info, c3, wgu_bf, wgu_bf,
      b_gu.reshape(E, 1, F2), b_gu.reshape(E, 1, F2), wdn_bf, b_dn.reshape(E, 1, D))


def _moe_plan(idx, rank, counts, tile):
    T, K = idx.shape
    E = counts.shape[0]
    n_rows = T * K
    nt = -(-n_rows // tile) + E + 1
    tiles_per = (counts + tile - 1) // tile
    tile_start = jnp.cumsum(tiles_per) - tiles_per
    pos = ((tile_start[idx] + 1) * tile + rank).reshape(-1)
    R = (nt + 1) * tile
    pad_info = n_rows + jnp.arange(R, dtype=I32) % tile
    info = pad_info.at[pos].set(jnp.arange(n_rows, dtype=I32))
    tid = jnp.arange(nt, dtype=I32)
    tile_end = tile_start + tiles_per
    te = jnp.sum((tid[:, None] >= tile_end[None, :]).astype(I32), axis=1)
    used = te < E
    te = jnp.minimum(te, E - 1)
    first = tile_start[te]
    nvalid = jnp.clip(counts[te] - (tid - first) * tile, 0, tile)
    nvalid = jnp.where(used, nvalid, 0).astype(I32)
    return te.astype(I32), nvalid, info.reshape(nt + 1, 1, tile)


def _ple_kernel(h_ref, s0, s1, s2, s3, w_ref, p_ref, gple_ref, wg_ref, bg_ref, wp_ref, gfin_ref,
                y_ref, tmp_scr):
    tm, D = h_ref.shape
    h = h_ref[...]
    w = w_ref[...]
    for k, s in enumerate((s0, s1, s2, s3)):
        tmp_scr[...] = s[...].reshape(tm, D)
        h = h + w[:, k:k + 1] * tmp_scr[...]
    gate = jax.nn.sigmoid(_bdot(_rms(h, gple_ref[...], NORM_EPS), wg_ref[...]) + bg_ref[...])
    h = h + gate * _bdot(p_ref[...], wp_ref[...])
    y_ref[...] = _rms(h, gfin_ref[...], NORM_EPS)


def _ple(h, slots, plane, wgt, tok0, p, g_ple, wg_bf, b_g, wp_bf, g_final, tm):
    T, D = h.shape
    PD = p.shape[1]
    row = lambda i: (i, 0)
    full = lambda a: pl.BlockSpec(a.shape, lambda i: (0,) * a.ndim)
    assert tok0 % tm == 0 and plane % tm == 0

    def slot_spec(k):
        return pl.BlockSpec((tm, 1, D), lambda i: ((k * plane + tok0) // tm + i, 0, 0))

    vecD = pl.BlockSpec((1, D), lambda i: (0, 0))
    return pl.pallas_call(
        _ple_kernel,
        grid=(T // tm,),
        in_specs=[pl.BlockSpec((tm, D), row)] + [slot_spec(k) for k in range(TOP_K)]
                 + [pl.BlockSpec((tm, LANES), row),
                    pl.BlockSpec((tm, PD), row), vecD, full(wg_bf), vecD, full(wp_bf), vecD],
        out_specs=pl.BlockSpec((tm, D), row),
        out_shape=jax.ShapeDtypeStruct((T, D), F32),
        scratch_shapes=[pltpu.VMEM((tm, D), F32)],
        compiler_params=_params(("parallel",)),
        name="ple_final",
    )(h, slots, slots, slots, slots, wgt, p, g_ple.reshape(1, D), wg_bf, b_g.reshape(1, D), wp_bf,
      g_final.reshape(1, D))


def kernel(x_prompt, x_sample, p_prompt, p_sample, cache_k, cache_v, state_ssm_re, state_ssm_im,
           page_table, g_mix, w_in, lambda_q1, lambda_k1, lambda_q2, lambda_k2, g_sub, a_re, a_im,
           log_dt, b_re, b_im, c_re, c_im, d_skip, w_glu, b_glu, g_ssm_out, w_o, g_ffn, w_router,
           b_router, w_gu, b_gu, w_dn, b_dn, g_ple, w_ple_gate, b_ple_gate, w_ple_proj, g_final):
    B, S, D = x_prompt.shape
    DB, DS = x_sample.shape[:2]
    assert DS == 1 and w_in.shape[0] == 1
    Tp, Ts = B * S, DB * DS
    past_len = page_table.shape[1] * PAGE_SIZE
    lam_init = 0.8 - 0.6 * math.exp(-0.3 * 0)
    G, P = a_re.shape[1:]
    H = b_re.shape[-1]
    AW = N_HEADS * HEAD_DIM
    SW = G * H

    lams = tuple(v[0].reshape(1, SUB_DIM) for v in (lambda_q1, lambda_k1, lambda_q2, lambda_k2))
    w_in_bf = w_in[0].astype(BF16)
    wo_bf = w_o[0].astype(BF16)
    wgu_bf = w_gu[0].astype(BF16)
    wdn_bf = w_dn[0].astype(BF16)
    wpg_bf = w_ple_gate[0].astype(BF16)
    wpp_bf = w_ple_proj[0].astype(BF16)
    wr_pad = jnp.zeros((D, LANES), F32).at[:, :N_EXPERTS].set(w_router[0])
    br_pad = jnp.zeros((1, LANES), F32).at[:, :N_EXPERTS].set(b_router[0][None, :])

    ab_re, ab_im, bb_re_t, bb_im_t = _discretize(a_re[0], a_im[0], log_dt[0], b_re[0], b_im[0])
    s5c = _s5_constants(ab_re, ab_im, bb_re_t, bb_im_t, c_re[0], c_im[0], d_skip[0],
                        w_glu[0], b_glu[0], g_ssm_out[0])

    def front(x2, pos, tm):
        tabs = _rope_tables(pos)
        return _inproj(x2, g_mix[0], w_in_bf, tabs, tm)

    tm_p = _tile(S, 512)
    xp = x_prompt.reshape(Tp, D)
    qb, k_p, kb, v_p, vb, u_p = front(xp, jnp.arange(S), tm_p)
    ta = _tile(S, 512)
    o_p = _attn_prompt(qb, kb, vb, lams, g_sub[0], B, S, lam_init, ta, ta)
    zeros_state = jnp.zeros((B, G * P), F32)
    L = _tile(S, max(1, 256 // B))
    m_p, sre_p, sim_p = _s5(u_p.reshape(B, S, SW), zeros_state, zeros_state, s5c, L)
    tm_r = _tile(Tp, 256)
    T_all = Tp + Ts
    assert Tp % Ts == 0
    c_all = jnp.zeros((T_all, 1, D), F32)
    h_p, c_all, idx_p, wgt_p, rank_p, cnt_p = _mix_router(
        xp, o_p, m_p.reshape(Tp, SW), wo_bf, g_ffn[0], wr_pad, br_pad, tm_r, c_all, 0)

    xs = x_sample.reshape(Ts, D)
    qs, k_s, _, v_s, _, u_s = front(xs, jnp.full((Ts,), past_len, I32), Ts)
    o_s = _attn_sample(qs, k_s, v_s, cache_k[0], cache_v[0], page_table, lams, g_sub[0], lam_init)
    m_s, sre_s, sim_s = _s5(u_s.reshape(DB, 1, SW), state_ssm_re[0].reshape(DB, G * P),
                            state_ssm_im[0].reshape(DB, G * P), s5c, 1)
    h_s, c_all, idx_s, wgt_s, rank_s, cnt_s = _mix_router(
        xs, o_s, m_s.reshape(Ts, SW), wo_bf, g_ffn[0], wr_pad, br_pad, Ts, c_all, Tp)

    cnt_p = cnt_p[0, :N_EXPERTS]
    counts = cnt_p + cnt_s[0, :N_EXPERTS]
    idx = jnp.concatenate([idx_p[:, :TOP_K], idx_s[:, :TOP_K]], axis=0)
    rank = jnp.concatenate([rank_p[:, :TOP_K], rank_s[:, :TOP_K] + cnt_p[idx_s[:, :TOP_K]]], axis=0)
    tile = 512
    tm_f = _tile(Tp, 256)
    assert tm_f % Ts == 0
    plane = -(-T_all // tm_f) * tm_f
    te, nvalid, info = _moe_plan(idx, rank, counts, tile)
    slots = _moe(c_all, te, nvalid, info, wgu_bf, b_gu[0], wdn_bf, b_dn[0], plane, tile,
                 _tile(w_dn.shape[2], 512))

    y_p = _ple(h_p, slots, plane, wgt_p, 0, p_prompt[0].reshape(Tp, -1), g_ple[0], wpg_bf,
               b_ple_gate[0], wpp_bf, g_final, tm_f)
    y_s = _ple(h_s, slots, plane, wgt_s, Tp, p_sample[0].reshape(Ts, -1), g_ple[0], wpg_bf,
               b_ple_gate[0], wpp_bf, g_final, Ts)

    NK = 2 * N_HEADS
    return (y_p.reshape(B, S, D), y_s.reshape(DB, DS, D),
            k_p.reshape(1, B, S, NK, SUB_DIM), v_p.reshape(1, B, S, N_HEADS, HEAD_DIM),
            sre_p.reshape(1, B, G, P), sim_p.reshape(1, B, G, P),
            k_s.reshape(1, DB, DS, NK, SUB_DIM), v_s.reshape(1, DB, DS, N_HEADS, HEAD_DIM),
            sre_s.reshape(1, DB, G, P), sim_s.reshape(1, DB, G, P))
```

```python
import functools
import math

import jax
import jax.numpy as jnp
from jax import lax
from jax.experimental import pallas as pl
from jax.experimental.pallas import tpu as pltpu

F32 = jnp.float32
BF16 = jnp.bfloat16
I32 = jnp.int32

N_HEADS = 8
HEAD_DIM = 128
SUB_DIM = 64
ROT_DIM = 16
ROPE_THETA = 500000.0
SSM_GROUP = 16
SSM_STATE = 64
N_EXPERTS = 32
TOP_K = 4
SWIGLU_LIMIT = 7.0
SWIGLU_ALPHA = 1.702
PAGE_SIZE = 128
NORM_EPS = 1e-6
SUBLN_EPS = 1e-5
LANES = 128
NEG_BIG = -1e30
VMEM_LIMIT = 56 * 1024 * 1024
PAGES_PER_STEP = 8


def _tile(n, pref):
    t = min(n, pref)
    while n % t:
        t -= 1
    return t


def _params(sem, **kw):
    return pltpu.CompilerParams(dimension_semantics=sem, vmem_limit_bytes=VMEM_LIMIT, **kw)


def _rms(x, g, eps):
    return x * lax.rsqrt(jnp.mean(x * x, axis=-1, keepdims=True) + eps) * g


def _bdot(a, b):
    return jnp.dot(a.astype(BF16), b.astype(BF16), preferred_element_type=F32)


def _disc_kernel(are_ref, aim_ref, ldt_ref, bre_ref, bim_ref, abre_ref, abim_ref, bbre_ref, bbim_ref):
    a_re = are_ref[...]
    a_im = aim_ref[...]
    dt = jnp.exp(ldt_ref[...])
    mag = jnp.exp(a_re * dt)
    ab_re = mag * jnp.cos(a_im * dt)
    ab_im = mag * jnp.sin(a_im * dt)
    nr, ni = ab_re - 1.0, ab_im
    den = a_re * a_re + a_im * a_im
    f_re = (nr * a_re + ni * a_im) / den
    f_im = (ni * a_re - nr * a_im) / den
    abre_ref[...] = ab_re
    abim_ref[...] = ab_im
    b_re = bre_ref[...]
    b_im = bim_ref[...]
    bbre_ref[...] = f_re * b_re - f_im * b_im
    bbim_ref[...] = f_re * b_im + f_im * b_re


def _discretize(a_re, a_im, log_dt, b_re, b_im):
    G, P = a_re.shape
    H = b_re.shape[-1]
    n = G * P
    row = lambda x: x.reshape(1, n)
    ldt = jnp.broadcast_to(log_dt[:, None], (G, P))
    bt = lambda x: jnp.transpose(x, (2, 0, 1)).reshape(H, n)
    out = pl.pallas_call(
        _disc_kernel,
        out_shape=(jax.ShapeDtypeStruct((1, n), F32), jax.ShapeDtypeStruct((1, n), F32),
                   jax.ShapeDtypeStruct((H, n), F32), jax.ShapeDtypeStruct((H, n), F32)),
        name="s5_discretize",
    )(row(a_re), row(a_im), row(ldt), bt(b_re), bt(b_im))
    return out


def _inproj_kernel(x_ref, g_ref, w_ref, cos_ref, sa_ref, sb_ref,
                   q_ref, k_ref, kb_ref, v_ref, vb_ref, u_ref, a_scr):
    j = pl.program_id(1)

    @pl.when(j == 0)
    def _():
        a_scr[...] = _rms(x_ref[...], g_ref[...], NORM_EPS).astype(BF16)

    acc = jnp.dot(a_scr[...], w_ref[...], preferred_element_type=F32)
    width = acc.shape[1]

    def rope_block(c):
        blk = acc[:, c * LANES:(c + 1) * LANES]
        return (blk * cos_ref[...] + pltpu.roll(blk, LANES - ROT_DIM // 2, 1) * sa_ref[...]
                + pltpu.roll(blk, ROT_DIM // 2, 1) * sb_ref[...])

    @pl.when(j == 0)
    def _():
        scale = SUB_DIM ** -0.5
        for c in range(width // LANES):
            q_ref[:, c * LANES:(c + 1) * LANES] = (rope_block(c) * scale).astype(BF16)

    @pl.when(j == 1)
    def _():
        for c in range(width // LANES):
            r = rope_block(c)
            k_ref[:, c * LANES:(c + 1) * LANES] = r
            kb_ref[:, c * LANES:(c + 1) * LANES] = r.astype(BF16)

    @pl.when(j == 2)
    def _():
        v_ref[...] = acc
        vb_ref[...] = acc.astype(BF16)

    @pl.when(j == 3)
    def _():
        u_ref[...] = acc


def _rope_tables(pos):
    half = ROT_DIM // 2
    inv = ROPE_THETA ** (-(jnp.arange(0, ROT_DIM, 2, dtype=F32) / ROT_DIM))
    ang = pos.astype(F32)[:, None] * inv[None, :]
    cos, sin = jnp.cos(ang), jnp.sin(ang)
    n = pos.shape[0]
    pad = jnp.zeros((n, SUB_DIM - ROT_DIM), F32)
    zer = jnp.zeros((n, half), F32)
    cos64 = jnp.concatenate([cos, cos, jnp.ones((n, SUB_DIM - ROT_DIM), F32)], axis=1)
    sa64 = jnp.concatenate([-sin, zer, pad], axis=1)
    sb64 = jnp.concatenate([zer, sin, pad], axis=1)
    rep = LANES // SUB_DIM
    return tuple(jnp.tile(t, (1, rep)) for t in (cos64, sa64, sb64))


def _inproj(x, g_mix, w_in_bf, tabs, tm):
    T, D = x.shape
    W = w_in_bf.shape[1] // 4
    ntab = tabs[0].shape[0] // tm
    row = lambda i, j: (i, 0)
    tab_spec = pl.BlockSpec((tm, LANES), lambda i, j: (i % ntab, 0))
    out_spec = pl.BlockSpec((tm, W), row)
    return pl.pallas_call(
        _inproj_kernel,
        grid=(T // tm, 4),
        in_specs=[pl.BlockSpec((tm, D), row),
                  pl.BlockSpec((1, D), lambda i, j: (0, 0)),
                  pl.BlockSpec((D, W), lambda i, j: (0, j)),
                  tab_spec, tab_spec, tab_spec],
        out_specs=(out_spec,) * 6,
        out_shape=(jax.ShapeDtypeStruct((T, W), BF16), jax.ShapeDtypeStruct((T, W), F32),
                   jax.ShapeDtypeStruct((T, W), BF16), jax.ShapeDtypeStruct((T, W), F32),
                   jax.ShapeDtypeStruct((T, W), BF16), jax.ShapeDtypeStruct((T, W), F32)),
        scratch_shapes=[pltpu.VMEM((tm, D), BF16)],
        compiler_params=_params(("parallel", "arbitrary")),
        name="inproj",
    )(x, g_mix.reshape(1, D), w_in_bf, *tabs)


def _lambda_value(lq1, lk1, lq2, lk2, lam_init):
    return (jnp.exp(jnp.sum(lq1[...] * lk1[...], axis=-1, keepdims=True))
            - jnp.exp(jnp.sum(lq2[...] * lk2[...], axis=-1, keepdims=True)) + lam_init)


def _attn_kernel(lq1, lk1, lq2, lk2, gsub_ref, q_ref, k_ref, v_ref, o_ref,
                 m_scr, l_scr, acc_scr, *, tq, tk, lam_init):
    qi = pl.program_id(1)
    ki = pl.program_id(2)

    @pl.when(ki == 0)
    def _():
        m_scr[...] = jnp.full(m_scr.shape, NEG_BIG, F32)
        l_scr[...] = jnp.zeros(l_scr.shape, F32)
        acc_scr[...] = jnp.zeros(acc_scr.shape, F32)

    last_k = (qi * tq + tq - 1) // tk

    @pl.when(ki <= last_k)
    def _():
        row = lax.broadcasted_iota(I32, (2 * tq, tk), 0)
        row = jnp.where(row >= tq, row - tq, row) + qi * tq
        col = lax.broadcasted_iota(I32, (2 * tq, tk), 1) + ki * tk
        visible = row >= col
        lane = lax.broadcasted_iota(I32, (tq, LANES), 1)
        for h in range(N_HEADS):
            sl = slice(h * HEAD_DIM, (h + 1) * HEAD_DIM)
            qh = q_ref[:, sl]
            zero = jnp.zeros_like(qh)
            q2 = jnp.concatenate([jnp.where(lane < SUB_DIM, qh, zero),
                                  jnp.where(lane >= SUB_DIM, qh, zero)], axis=0)
            s = lax.dot_general(q2, k_ref[:, sl], (((1,), (1,)), ((), ())),
                                preferred_element_type=F32)
            s = jnp.where(visible, s, NEG_BIG)
            m_old = m_scr[h]
            m_new = jnp.maximum(m_old, jnp.max(s, axis=1, keepdims=True))
            p = jnp.exp(s - m_new)
            alpha = jnp.exp(m_old - m_new)
            l_scr[h] = alpha * l_scr[h] + jnp.sum(p, axis=1, keepdims=True)
            acc_scr[h] = alpha * acc_scr[h] + jnp.dot(p.astype(BF16), v_ref[:, sl],
                                                      preferred_element_type=F32)
            m_scr[h] = m_new

    @pl.when(ki == pl.num_programs(2) - 1)
    def _():
        lam = _lambda_value(lq1, lk1, lq2, lk2, lam_init)
        for h in range(N_HEADS):
            a = acc_scr[h] / l_scr[h]
            o = a[:tq] - lam * a[tq:]
            o = _rms(o, gsub_ref[...], SUBLN_EPS) * (1.0 - lam_init)
            o_ref[:, h * HEAD_DIM:(h + 1) * HEAD_DIM] = o.astype(BF16)


def _attn_prompt(qb, kb, vb, lams, g_sub, nb, S, lam_init, tq, tk):
    T, W = qb.shape
    nq, nk = S // tq, S // tk
    vec = pl.BlockSpec((1, SUB_DIM), lambda b, i, j: (0, 0))

    def kv_map(b, i, j):
        return (b * nk + jnp.minimum(j, (i * tq + tq - 1) // tk), 0)

    return pl.pallas_call(
        functools.partial(_attn_kernel, tq=tq, tk=tk, lam_init=lam_init),
        grid=(nb, nq, nk),
        in_specs=[vec, vec, vec, vec,
                  pl.BlockSpec((1, HEAD_DIM), lambda b, i, j: (0, 0)),
                  pl.BlockSpec((tq, W), lambda b, i, j: (b * nq + i, 0)),
                  pl.BlockSpec((tk, W), kv_map),
                  pl.BlockSpec((tk, W), kv_map)],
        out_specs=pl.BlockSpec((tq, W), lambda b, i, j: (b * nq + i, 0)),
        out_shape=jax.ShapeDtypeStruct((T, W), BF16),
        scratch_shapes=[pltpu.VMEM((N_HEADS, 2 * tq, 1), F32),
                        pltpu.VMEM((N_HEADS, 2 * tq, 1), F32),
                        pltpu.VMEM((N_HEADS, 2 * tq, HEAD_DIM), F32)],
        compiler_params=_params(("parallel", "parallel", "arbitrary")),
        name="attn_prompt",
    )(*lams, g_sub.reshape(1, HEAD_DIM), qb, kb, vb)


def _attn_sample_kernel(pt_ref, lq1, lk1, lq2, lk2, gsub_ref, qbd_ref, q_ref, kn_ref, vn_ref,
                        *rest, lam_init):
    npg = PAGES_PER_STEP
    k_refs = rest[:npg]
    v_refs = rest[npg:2 * npg]
    o_ref, m_scr, l_scr, acc_scr, o_scr = rest[2 * npg:]
    j = pl.program_id(1)
    nh = N_HEADS
    ns = 2 * nh

    @pl.when(j == 0)
    def _():
        m_scr[...] = jnp.full(m_scr.shape, NEG_BIG, F32)
        l_scr[...] = jnp.zeros(l_scr.shape, F32)
        acc_scr[...] = jnp.zeros(acc_scr.shape, F32)

    qbd = qbd_ref[0].astype(BF16)
    s = jnp.concatenate([jnp.dot(qbd, k_refs[i][...].astype(BF16), preferred_element_type=F32)
                         for i in range(npg)], axis=1)
    m_old = m_scr[...]
    m_new = jnp.maximum(m_old, jnp.max(s, axis=1, keepdims=True))
    p = jnp.exp(s - m_new)
    alpha = jnp.exp(m_old - m_new)
    l_scr[...] = alpha * l_scr[...] + jnp.sum(p, axis=1, keepdims=True)
    pb = p.astype(BF16)
    row_head = lax.broadcasted_iota(I32, (ns, HEAD_DIM), 0) // 2
    pv = jnp.zeros((ns, HEAD_DIM), F32)
    for h in range(nh):
        v_h = jnp.concatenate([v_refs[i][pl.ds(h, PAGE_SIZE, stride=nh), :] for i in range(npg)],
                              axis=0).astype(BF16)
        pv = pv + jnp.where(row_head == h, jnp.dot(pb, v_h, preferred_element_type=F32), 0.0)
    acc_scr[...] = alpha * acc_scr[...] + pv
    m_scr[...] = m_new

    @pl.when(j == pl.num_programs(1) - 1)
    def _():
        qs = q_ref[0].astype(BF16).astype(F32)
        kn = kn_ref[0].astype(BF16).astype(F32)
        vn = vn_ref[0].astype(BF16).astype(F32)
        s_new = jnp.sum(qs * kn, axis=1, keepdims=True)
        m_o = m_scr[...]
        m_n = jnp.maximum(m_o, s_new)
        a = jnp.exp(m_o - m_n)
        p_n = jnp.exp(s_new - m_n)
        l = a * l_scr[...] + p_n
        acc = a * acc_scr[...] + p_n.astype(BF16).astype(F32) * vn
        o_scr[...] = acc / l
        lam = _lambda_value(lq1, lk1, lq2, lk2, lam_init)
        o8 = o_scr[pl.ds(0, nh, stride=2), :] - lam * o_scr[pl.ds(1, nh, stride=2), :]
        o_ref[0] = _rms(o8, gsub_ref[...], SUBLN_EPS) * (1.0 - lam_init)


def _attn_sample(qb, k_new, v_new, cache_k, cache_v, page_table, lams, g_sub, lam_init):
    nb, W = qb.shape
    n_pool = cache_k.shape[0]
    n_pages = page_table.shape[1]
    npg = PAGES_PER_STEP
    nh = N_HEADS
    assert n_pages % npg == 0
    ns = 2 * nh
    ck = jnp.transpose(cache_k, (0, 2, 3, 1)).reshape(n_pool, ns * SUB_DIM, PAGE_SIZE)
    cv = cache_v.reshape(n_pool, PAGE_SIZE * nh, HEAD_DIM)
    q16 = qb.astype(F32).reshape(nb, ns, SUB_DIM)
    qbd = (q16[:, :, None, :] * jnp.eye(ns, dtype=F32)[None, :, :, None]).reshape(nb, ns, ns * SUB_DIM)
    vn16 = jnp.repeat(v_new.reshape(nb, nh, HEAD_DIM), 2, axis=1)
    vec = pl.BlockSpec((1, SUB_DIM), lambda b, j, pt: (0, 0))
    per_seq = lambda a: pl.BlockSpec((1,) + a.shape[1:], lambda b, j, pt: (b, 0, 0))

    def page_spec(i, rows, width):
        return pl.BlockSpec((None, rows, width), lambda b, j, pt: (pt[b, j * npg + i], 0, 0))

    k16 = k_new.reshape(nb, ns, SUB_DIM)
    grid_spec = pltpu.PrefetchScalarGridSpec(
        num_scalar_prefetch=1,
        grid=(nb, n_pages // npg),
        in_specs=[vec, vec, vec, vec,
                  pl.BlockSpec((1, HEAD_DIM), lambda b, j, pt: (0, 0)),
                  per_seq(qbd), per_seq(q16), per_seq(k16), per_seq(vn16)]
                 + [page_spec(i, ns * SUB_DIM, PAGE_SIZE) for i in range(npg)]
                 + [page_spec(i, PAGE_SIZE * nh, HEAD_DIM) for i in range(npg)],
        out_specs=pl.BlockSpec((1, nh, HEAD_DIM), lambda b, j, pt: (b, 0, 0)),
        scratch_shapes=[pltpu.VMEM((ns, 1), F32), pltpu.VMEM((ns, 1), F32),
                        pltpu.VMEM((ns, HEAD_DIM), F32), pltpu.VMEM((ns, HEAD_DIM), F32)],
    )
    out = pl.pallas_call(
        functools.partial(_attn_sample_kernel, lam_init=lam_init),
        grid_spec=grid_spec,
        out_shape=jax.ShapeDtypeStruct((nb, nh, HEAD_DIM), F32),
        compiler_params=_params(("parallel", "arbitrary")),
        name="attn_sample",
    )(page_table, *lams, g_sub.reshape(1, HEAD_DIM), qbd, q16, k16, vn16, *([ck] * npg), *([cv] * npg))
    return out.reshape(nb, W).astype(BF16)


def _s5_kernel(u_ref, h0re_ref, h0im_ref, are_ref, aim_ref, bc_ref, cre_ref, cim_ref, d_ref,
               wglu_ref, bglu_ref, gout_ref,
               m_ref, stre_ref, stim_ref,
               utm_scr, xre_scr, xim_scr, sre_scr, sim_scr, *, nb, L, slab):
    c = pl.program_id(0)
    W = utm_scr.shape[1]
    NS = xre_scr.shape[1]
    kt = bc_ref.shape[0]
    cw = W // kt
    sw = NS // kt

    @pl.when(c == 0)
    def _():
        sre_scr[...] = h0re_ref[...]
        sim_scr[...] = h0im_ref[...]

    for t in range(L):
        utm_scr[t * nb:(t + 1) * nb, :] = u_ref[:, t:t + 1, :].reshape(nb, W)

    for k in range(kt):
        bu = _bdot(utm_scr[:, k * cw:(k + 1) * cw], bc_ref[k])
        xre_scr[:, k * sw:(k + 1) * sw] = bu[:, :sw]
        xim_scr[:, k * sw:(k + 1) * sw] = bu[:, sw:]

    def slab_body(j, carry):
        off = pl.multiple_of(j * slab, slab)
        lanes = pl.ds(off, slab)
        a_re = jnp.broadcast_to(are_ref[:, lanes], (nb, slab))
        a_im = jnp.broadcast_to(aim_ref[:, lanes], (nb, slab))
        s_re = sre_scr[:, lanes]
        s_im = sim_scr[:, lanes]
        for t in range(L):
            rows = slice(t * nb, (t + 1) * nb)
            n_re = a_re * s_re - a_im * s_im + xre_scr[rows, lanes]
            n_im = a_re * s_im + a_im * s_re + xim_scr[rows, lanes]
            xre_scr[rows, lanes] = n_re
            xim_scr[rows, lanes] = n_im
            s_re, s_im = n_re, n_im
        sre_scr[:, lanes] = s_re
        sim_scr[:, lanes] = s_im
        return carry

    lax.fori_loop(0, NS // slab, slab_body, 0)

    ys = []
    for k in range(kt):
        ys.append(_bdot(xre_scr[:, k * sw:(k + 1) * sw], cre_ref[k])
                  - _bdot(xim_scr[:, k * sw:(k + 1) * sw], cim_ref[k]))
    y = jnp.concatenate(ys, axis=1) + d_ref[...] * utm_scr[...]
    z = 0.5 * y * (1.0 + lax.erf(y * (2.0 ** -0.5)))
    gate = jax.nn.sigmoid(_bdot(z, wglu_ref[...]) + bglu_ref[...])
    m = _rms(z * gate, gout_ref[...], NORM_EPS)
    for t in range(L):
        m_ref[:, t:t + 1, :] = m[t * nb:(t + 1) * nb, :].reshape(nb, 1, W)

    @pl.when(c == pl.num_programs(0) - 1)
    def _():
        stre_ref[...] = sre_scr[...]
        stim_ref[...] = sim_scr[...]


def _s5(u3, h0_re, h0_im, consts, L):
    nb, S, W = u3.shape
    ab_re, ab_im, bc, cre, cim, d_row, wglu_bf, bglu, gout = consts
    NS = ab_re.shape[1]
    kt = bc.shape[0]
    slab = 512 if nb <= 16 else 256
    full2 = lambda a: pl.BlockSpec(a.shape, lambda c: (0, 0))
    full3 = lambda a: pl.BlockSpec(a.shape, lambda c: (0, 0, 0))
    rows = nb * L
    return pl.pallas_call(
        functools.partial(_s5_kernel, nb=nb, L=L, slab=slab),
        grid=(S // L,),
        in_specs=[pl.BlockSpec((nb, L, W), lambda c: (0, c, 0)),
                  full2(h0_re), full2(h0_im), full2(ab_re), full2(ab_im),
                  full3(bc), full3(cre), full3(cim), full2(d_row),
                  full2(wglu_bf), full2(bglu), full2(gout)],
        out_specs=(pl.BlockSpec((nb, L, W), lambda c: (0, c, 0)),
                   pl.BlockSpec((nb, NS), lambda c: (0, 0)),
                   pl.BlockSpec((nb, NS), lambda c: (0, 0))),
        out_shape=(jax.ShapeDtypeStruct((nb, S, W), F32),
                   jax.ShapeDtypeStruct((nb, NS), F32), jax.ShapeDtypeStruct((nb, NS), F32)),
        scratch_shapes=[pltpu.VMEM((rows, W), F32), pltpu.VMEM((rows, NS), F32),
                        pltpu.VMEM((rows, NS), F32), pltpu.VMEM((nb, NS), F32),
                        pltpu.VMEM((nb, NS), F32)],
        compiler_params=_params(("arbitrary",)),
        name="s5_scan_glu",
    )(u3, h0_re, h0_im, ab_re, ab_im, bc, cre, cim, d_row, wglu_bf, bglu, gout)


def _s5_constants(ab_re, ab_im, bb_re_t, bb_im_t, c_re, c_im, d_skip, w_glu, b_glu, g_ssm_out):
    G, H, P = c_re.shape
    gt = 256 // H
    kt = G // gt
    eye = jnp.eye(gt, dtype=F32)

    def b_tiles(bt):
        b = bt.reshape(H, kt, gt, P).transpose(1, 2, 0, 3)
        return (b[:, :, :, None, :] * eye[None, :, None, :, None]).reshape(kt, gt * H, gt * P)

    def c_tiles(cm):
        c = cm.reshape(kt, gt, H, P).transpose(0, 1, 3, 2)
        return (c[:, :, :, None, :] * eye[None, :, None, :, None]).reshape(kt, gt * P, gt * H)

    bc = jnp.concatenate([b_tiles(bb_re_t), b_tiles(bb_im_t)], axis=2).astype(BF16)
    W = G * H
    return (ab_re, ab_im, bc, c_tiles(c_re).astype(BF16), c_tiles(c_im).astype(BF16),
            d_skip.reshape(1, W), w_glu.astype(BF16), b_glu.reshape(1, W), g_ssm_out.reshape(1, W))


def _mix_router_kernel(x_ref, o_ref, m_ref, wo_ref, gffn_ref, wr_ref, br_ref, cbuf_ref,
                       h_ref, c_ref, idx_ref, wgt_ref, rank_ref, cnt_ref, cnt_scr, *, tm):
    del cbuf_ref
    i = pl.program_id(0)
    aw = o_ref.shape[1]

    @pl.when(i == 0)
    def _():
        cnt_scr[...] = jnp.zeros(cnt_scr.shape, F32)

    mix = (jnp.dot(o_ref[...], wo_ref[:aw, :], preferred_element_type=F32)
           + _bdot(m_ref[...], wo_ref[aw:, :]))
    h = x_ref[...] + mix
    h_ref[...] = h
    c = _rms(h, gffn_ref[...], NORM_EPS)
    c_ref[...] = c.reshape(c_ref.shape)

    logits = jnp.dot(c, wr_ref[...], preferred_element_type=F32,
                     precision=lax.Precision.HIGHEST) + br_ref[...]
    lane = lax.broadcasted_iota(I32, (tm, LANES), 1)
    lanef = lane.astype(F32)
    work = jnp.where(lane < N_EXPERTS, logits, -jnp.inf)
    idx_out = jnp.zeros((tm, LANES), F32)
    val_out = jnp.zeros((tm, LANES), F32)
    sel = jnp.zeros((tm, LANES), F32)
    picks = []
    for k in range(TOP_K):
        vmax = jnp.max(work, axis=1, keepdims=True)
        imax = jnp.min(jnp.where(work == vmax, lanef, float(LANES)), axis=1, keepdims=True)
        hit = lanef == imax
        picks.append(hit)
        sel = jnp.where(hit, 1.0, sel)
        work = jnp.where(hit, -jnp.inf, work)
        idx_out = jnp.where(lane == k, imax, idx_out)
        val_out = jnp.where(lane == k, vmax, val_out)
    v0 = val_out[:, 0:1]
    e = jnp.where(lane < TOP_K, jnp.exp(val_out - v0), 0.0)
    wgt_ref[...] = e / jnp.sum(e, axis=1, keepdims=True)
    idx_ref[...] = idx_out.astype(I32)
    r = lax.broadcasted_iota(I32, (tm, tm), 0)
    q = lax.broadcasted_iota(I32, (tm, tm), 1)
    tril = jnp.where(r > q, 1.0, 0.0).astype(BF16)
    before = jnp.dot(tril, sel.astype(BF16), preferred_element_type=F32) + cnt_scr[...]
    rank = jnp.zeros((tm, LANES), F32)
    for k in range(TOP_K):
        rk = jnp.sum(jnp.where(picks[k], before, 0.0), axis=1, keepdims=True)
        rank = jnp.where(lane == k, rk, rank)
    rank_ref[...] = rank.astype(I32)
    cnt_scr[...] = cnt_scr[...] + jnp.sum(sel, axis=0, keepdims=True)
    cnt_ref[...] = cnt_scr[...].astype(I32)


def _mix_router(x, o, m, wo_bf, g_ffn, wr_pad, br_pad, tm, c_buf, c_row0):
    T, D = x.shape
    aw = o.shape[1]
    c_rows = c_buf.shape[0]
    assert c_row0 % tm == 0 and c_buf.shape == (c_rows, 1, D)
    row = lambda i: (i, 0)
    full = lambda a: pl.BlockSpec(a.shape, lambda i: (0,) * a.ndim)
    lane_out = pl.BlockSpec((tm, LANES), row)
    in_specs = [pl.BlockSpec((tm, D), row), pl.BlockSpec((tm, aw), row),
                pl.BlockSpec((tm, D - aw), row), full(wo_bf),
                pl.BlockSpec((1, D), lambda i: (0, 0)), full(wr_pad), full(br_pad),
                pl.BlockSpec(memory_space=pl.ANY)]
    args = [x, o, m, wo_bf, g_ffn.reshape(1, D), wr_pad, br_pad, c_buf]
    aliases = {7: 1}
    return pl.pallas_call(
        functools.partial(_mix_router_kernel, tm=tm),
        grid=(T // tm,),
        in_specs=in_specs,
        out_specs=(pl.BlockSpec((tm, D), row),
                   pl.BlockSpec((tm, 1, D), lambda i: (c_row0 // tm + i, 0, 0)),
                   lane_out, lane_out, lane_out, pl.BlockSpec((1, LANES), lambda i: (0, 0))),
        out_shape=(jax.ShapeDtypeStruct((T, D), F32), jax.ShapeDtypeStruct((c_rows, 1, D), F32),
                   jax.ShapeDtypeStruct((T, LANES), I32), jax.ShapeDtypeStruct((T, LANES), F32),
                   jax.ShapeDtypeStruct((T, LANES), I32), jax.ShapeDtypeStruct((1, LANES), I32)),
        scratch_shapes=[pltpu.VMEM((1, LANES), F32)],
        input_output_aliases=aliases,
        compiler_params=_params(("arbitrary",)),
        name="mix_router",
    )(*args)


def _moe_kernel(te_ref, tn_ref, info0_ref, infon_ref, infop_ref, c_hbm,
                wg_ref, wu_ref, bg_ref, bu_ref, wd_ref, bd_ref, slots_hbm,
                x_scr, y_scr, xb_scr, acc_scr, gsem, ssem, *, n_tok, plane):
    i = pl.program_id(0)
    f = pl.program_id(1)
    nf = pl.num_programs(1)
    tile, D = acc_scr.shape
    chunk = tile // nf
    used = tn_ref[i] > 0
    prev_used = jnp.logical_and(i > 0, tn_ref[jnp.maximum(i - 1, 0)] > 0)
    slot = i % 2
    other = 1 - slot

    def gather_start(info_ref, r, buf):
        pltpu.make_async_copy(c_hbm.at[info_ref[0, 0, r]], x_scr.at[buf, r], gsem.at[buf]).start()

    def gather_wait(buf):
        pltpu.make_async_copy(c_hbm.at[pl.ds(0, tile)], x_scr.at[buf], gsem.at[buf]).wait()

    def scatter_start(info_ref, r, buf):
        pltpu.make_async_copy(y_scr.at[buf, r], slots_hbm.at[info_ref[0, 0, r]], ssem.at[buf]).start()

    def scatter_wait(buf):
        pltpu.make_async_copy(y_scr.at[buf], slots_hbm.at[pl.ds(0, tile)], ssem.at[buf]).wait()

    def tail_copy(k):
        return pltpu.make_async_copy(y_scr.at[0, pl.ds(0, plane - n_tok)],
                                     slots_hbm.at[pl.ds(k * plane + n_tok, plane - n_tok)], ssem.at[0])

    @pl.when(jnp.logical_and(i == 0, f == 0))
    def _():
        y_scr[...] = jnp.zeros(y_scr.shape, F32)
        if plane > n_tok:
            for k in range(TOP_K):
                tail_copy(k).start()
            for k in range(TOP_K):
                tail_copy(k).wait()

        def body(r, carry):
            gather_start(info0_ref, r, 0)
            return carry

        lax.fori_loop(0, tile, body, 0)

    @pl.when(jnp.logical_and(f == 0, jnp.logical_or(i == 0, prev_used)))
    def _():
        gather_wait(slot)

    @pl.when(jnp.logical_and(f == 0, prev_used))
    def _():
        scatter_wait(slot)

    @pl.when(jnp.logical_and(f == 0, used))
    def _():
        acc_scr[...] = x_scr[slot].reshape(tile, D)
        xb_scr[...] = acc_scr[...].astype(BF16)
        acc_scr[...] = jnp.zeros((tile, D), F32)

    @pl.when(used)
    def _():
        x = xb_scr[...]
        gate = jnp.dot(x, wg_ref[0], preferred_element_type=F32) + bg_ref[0]
        up = jnp.dot(x, wu_ref[0], preferred_element_type=F32) + bu_ref[0]
        gate = jnp.minimum(gate, SWIGLU_LIMIT)
        up = jnp.clip(up, -SWIGLU_LIMIT, SWIGLU_LIMIT)
        hmid = (up + 1.0) * gate * jax.nn.sigmoid(SWIGLU_ALPHA * gate)
        acc_scr[...] += jnp.dot(hmid.astype(BF16), wd_ref[0], preferred_element_type=F32)
        for q in range(chunk):
            r = f * chunk + q
            gather_start(infon_ref, r, other)
            scatter_start(infop_ref, r, other)

    @pl.when(jnp.logical_and(f == nf - 1, used))
    def _():
        y_scr[slot] = (acc_scr[...] + bd_ref[0]).reshape(tile, 1, D)

    @pl.when(jnp.logical_and(f == 0, jnp.logical_and(prev_used, jnp.logical_not(used))))
    def _():
        def body(r, carry):
            scatter_start(infop_ref, r, other)
            return carry

        lax.fori_loop(0, tile, body, 0)
        scatter_wait(other)


def _moe(c3, tile_expert, tile_nvalid, info, wgu_bf, b_gu, wdn_bf, b_dn, plane, tile, tf):
    n_tok = c3.shape[0]
    n_rows = n_tok * TOP_K
    nt = info.shape[0] - 1
    E, D, F2 = wgu_bf.shape
    FF = F2 // 2
    nf = FF // tf
    assert tile % nf == 0 and 0 <= plane - n_tok <= tile
    src = jnp.minimum(info // TOP_K, n_tok - 1)
    dst = jnp.where(info < n_rows, (info % TOP_K) * plane + info // TOP_K, info + (TOP_K * plane - n_rows))

    def fblk(i, f, tn):
        return jnp.where(tn[i] > 0, f, nf - 1)

    smem_tile = lambda fn: pl.BlockSpec((1, 1, tile), fn, memory_space=pltpu.SMEM)
    grid_spec = pltpu.PrefetchScalarGridSpec(
        num_scalar_prefetch=2,
        grid=(nt, nf),
        in_specs=[
            smem_tile(lambda i, f, te, tn: (1, 0, 0)),
            smem_tile(lambda i, f, te, tn: (jnp.minimum(i + 2, nt), 0, 0)),
            smem_tile(lambda i, f, te, tn: (i, 0, 0)),
            pl.BlockSpec(memory_space=pl.ANY),
            pl.BlockSpec((1, D, tf), lambda i, f, te, tn: (te[i], 0, fblk(i, f, tn))),
            pl.BlockSpec((1, D, tf), lambda i, f, te, tn: (te[i], 0, nf + fblk(i, f, tn))),
            pl.BlockSpec((1, 1, tf), lambda i, f, te, tn: (te[i], 0, fblk(i, f, tn))),
            pl.BlockSpec((1, 1, tf), lambda i, f, te, tn: (te[i], 0, nf + fblk(i, f, tn))),
            pl.BlockSpec((1, tf, D), lambda i, f, te, tn: (te[i], fblk(i, f, tn), 0)),
            pl.BlockSpec((1, 1, D), lambda i, f, te, tn: (te[i], 0, 0)),
        ],
        out_specs=pl.BlockSpec(memory_space=pl.ANY),
        scratch_shapes=[pltpu.VMEM((2, tile, 1, D), F32), pltpu.VMEM((2, tile, 1, D), F32),
                        pltpu.VMEM((tile, D), BF16), pltpu.VMEM((tile, D), F32),
                        pltpu.SemaphoreType.DMA((2,)), pltpu.SemaphoreType.DMA((2,))],
    )
    return pl.pallas_call(
        functools.partial(_moe_kernel, n_tok=n_tok, plane=plane),
        grid_spec=grid_spec,
        out_shape=jax.ShapeDtypeStruct((TOP_K * plane + tile, 1, D), F32),
        compiler_params=_params(("arbitrary", "arbitrary")),
        name="moe_experts",
    )(tile_expert, tile_nvalid, src, src, dst, c3, wgu_bf, wgu_bf,
      b_gu.reshape(E, 1, F2), b_gu.reshape(E, 1, F2), wdn_bf, b_dn.reshape(E, 1, D))


def _moe_plan(idx, rank, counts, tile):
    T, K = idx.shape
    E = counts.shape[0]
    n_rows = T * K
    nt = -(-n_rows // tile) + E + 1
    tiles_per = (counts + tile - 1) // tile
    tile_start = jnp.cumsum(tiles_per) - tiles_per
    pos = ((tile_start[idx] + 1) * tile + rank).reshape(-1)
    R = (nt + 1) * tile
    pad_info = n_rows + jnp.arange(R, dtype=I32) % tile
    info = pad_info.at[pos].set(jnp.arange(n_rows, dtype=I32))
    tid = jnp.arange(nt, dtype=I32)
    tile_end = tile_start + tiles_per
    te = jnp.sum((tid[:, None] >= tile_end[None, :]).astype(I32), axis=1)
    used = te < E
    te = jnp.minimum(te, E - 1)
    first = tile_start[te]
    nvalid = jnp.clip(counts[te] - (tid - first) * tile, 0, tile)
    nvalid = jnp.where(used, nvalid, 0).astype(I32)
    return te.astype(I32), nvalid, info.reshape(nt + 1, 1, tile)


def _ple_kernel(h_ref, s0, s1, s2, s3, w_ref, p_ref, gple_ref, wg_ref, bg_ref, wp_ref, gfin_ref,
                y_ref, tmp_scr):
    tm, D = h_ref.shape
    h = h_ref[...]
    w = w_ref[...]
    for k, s in enumerate((s0, s1, s2, s3)):
        tmp_scr[...] = s[...].reshape(tm, D)
        h = h + w[:, k:k + 1] * tmp_scr[...]
    gate = jax.nn.sigmoid(_bdot(_rms(h, gple_ref[...], NORM_EPS), wg_ref[...]) + bg_ref[...])
    h = h + gate * _bdot(p_ref[...], wp_ref[...])
    y_ref[...] = _rms(h, gfin_ref[...], NORM_EPS)


def _ple(h, slots, plane, wgt, tok0, p, g_ple, wg_bf, b_g, wp_bf, g_final, tm):
    T, D = h.shape
    PD = p.shape[1]
    row = lambda i: (i, 0)
    full = lambda a: pl.BlockSpec(a.shape, lambda i: (0,) * a.ndim)
    assert tok0 % tm == 0 and plane % tm == 0

    def slot_spec(k):
        return pl.BlockSpec((tm, 1, D), lambda i: ((k * plane + tok0) // tm + i, 0, 0))

    vecD = pl.BlockSpec((1, D), lambda i: (0, 0))
    return pl.pallas_call(
        _ple_kernel,
        grid=(T // tm,),
        in_specs=[pl.BlockSpec((tm, D), row)] + [slot_spec(k) for k in range(TOP_K)]
                 + [pl.BlockSpec((tm, LANES), row),
                    pl.BlockSpec((tm, PD), row), vecD, full(wg_bf), vecD, full(wp_bf), vecD],
        out_specs=pl.BlockSpec((tm, D), row),
        out_shape=jax.ShapeDtypeStruct((T, D), F32),
        scratch_shapes=[pltpu.VMEM((tm, D), F32)],
        compiler_params=_params(("parallel",)),
        name="ple_final",
    )(h, slots, slots, slots, slots, wgt, p, g_ple.reshape(1, D), wg_bf, b_g.reshape(1, D), wp_bf,
      g_final.reshape(1, D))


def kernel(x_prompt, x_sample, p_prompt, p_sample, cache_k, cache_v, state_ssm_re, state_ssm_im,
           page_table, g_mix, w_in, lambda_q1, lambda_k1, lambda_q2, lambda_k2, g_sub, a_re, a_im,
           log_dt, b_re, b_im, c_re, c_im, d_skip, w_glu, b_glu, g_ssm_out, w_o, g_ffn, w_router,
           b_router, w_gu, b_gu, w_dn, b_dn, g_ple, w_ple_gate, b_ple_gate, w_ple_proj, g_final):
    B, S, D = x_prompt.shape
    DB, DS = x_sample.shape[:2]
    assert DS == 1 and w_in.shape[0] == 1
    Tp, Ts = B * S, DB * DS
    past_len = page_table.shape[1] * PAGE_SIZE
    lam_init = 0.8 - 0.6 * math.exp(-0.3 * 0)
    G, P = a_re.shape[1:]
    H = b_re.shape[-1]
    AW = N_HEADS * HEAD_DIM
    SW = G * H

    lams = tuple(v[0].reshape(1, SUB_DIM) for v in (lambda_q1, lambda_k1, lambda_q2, lambda_k2))
    w_in_bf = w_in[0].astype(BF16)
    wo_bf = w_o[0].astype(BF16)
    wgu_bf = w_gu[0].astype(BF16)
    wdn_bf = w_dn[0].astype(BF16)
    wpg_bf = w_ple_gate[0].astype(BF16)
    wpp_bf = w_ple_proj[0].astype(BF16)
    wr_pad = jnp.zeros((D, LANES), F32).at[:, :N_EXPERTS].set(w_router[0])
    br_pad = jnp.zeros((1, LANES), F32).at[:, :N_EXPERTS].set(b_router[0][None, :])

    ab_re, ab_im, bb_re_t, bb_im_t = _discretize(a_re[0], a_im[0], log_dt[0], b_re[0], b_im[0])
    s5c = _s5_constants(ab_re, ab_im, bb_re_t, bb_im_t, c_re[0], c_im[0], d_skip[0],
                        w_glu[0], b_glu[0], g_ssm_out[0])

    def front(x2, pos, tm):
        tabs = _rope_tables(pos)
        return _inproj(x2, g_mix[0], w_in_bf, tabs, tm)

    tm_p = _tile(S, 512)
    xp = x_prompt.reshape(Tp, D)
    qb, k_p, kb, v_p, vb, u_p = front(xp, jnp.arange(S), tm_p)
    ta = _tile(S, 512)
    o_p = _attn_prompt(qb, kb, vb, lams, g_sub[0], B, S, lam_init, ta, ta)
    zeros_state = jnp.zeros((B, G * P), F32)
    L = _tile(S, max(1, 256 // B))
    m_p, sre_p, sim_p = _s5(u_p.reshape(B, S, SW), zeros_state, zeros_state, s5c, L)
    tm_r = _tile(Tp, 256)
    T_all = Tp + Ts
    assert Tp % Ts == 0
    c_all = jnp.zeros((T_all, 1, D), F32)
    h_p, c_all, idx_p, wgt_p, rank_p, cnt_p = _mix_router(
        xp, o_p, m_p.reshape(Tp, SW), wo_bf, g_ffn[0], wr_pad, br_pad, tm_r, c_all, 0)

    xs = x_sample.reshape(Ts, D)
    qs, k_s, _, v_s, _, u_s = front(xs, jnp.full((Ts,), past_len, I32), Ts)
    o_s = _attn_sample(qs, k_s, v_s, cache_k[0], cache_v[0], page_table, lams, g_sub[0], lam_init)
    m_s, sre_s, sim_s = _s5(u_s.reshape(DB, 1, SW), state_ssm_re[0].reshape(DB, G * P),
                            state_ssm_im[0].reshape(DB, G * P), s5c, 1)
    h_s, c_all, idx_s, wgt_s, rank_s, cnt_s = _mix_router(
        xs, o_s, m_s.reshape(Ts, SW), wo_bf, g_ffn[0], wr_pad, br_pad, Ts, c_all, Tp)

    cnt_p = cnt_p[0, :N_EXPERTS]
    counts = cnt_p + cnt_s[0, :N_EXPERTS]
    idx = jnp.concatenate([idx_p[:, :TOP_K], idx_s[:, :TOP_K]], axis=0)
    rank = jnp.concatenate([rank_p[:, :TOP_K], rank_s[:, :TOP_K] + cnt_p[idx_s[:, :TOP_K]]], axis=0)
    tile = 512
    tm_f = _tile(Tp, 256)
    assert tm_f % Ts == 0
    plane = -(-T_all // tm_f) * tm_f
    te, nvalid, info = _moe_plan(idx, rank, counts, tile)
    slots = _moe(c_all, te, nvalid, info, wgu_bf, b_gu[0], wdn_bf, b_dn[0], plane, tile,
                 _tile(w_dn.shape[2], 512))

    y_p = _ple(h_p, slots, plane, wgt_p, 0, p_prompt[0].reshape(Tp, -1), g_ple[0], wpg_bf,
               b_ple_gate[0], wpp_bf, g_final, tm_f)
    y_s = _ple(h_s, slots, plane, wgt_s, Tp, p_sample[0].reshape(Ts, -1), g_ple[0], wpg_bf,
               b_ple_gate[0], wpp_bf, g_final, Ts)

    NK = 2 * N_HEADS
    return (y_p.reshape(B, S, D), y_s.reshape(DB, DS, D),
            k_p.reshape(1, B, S, NK, SUB_DIM), v_p.reshape(1, B, S, N_HEADS, HEAD_DIM),
            sre_p.reshape(1, B, G, P), sim_p.reshape(1, B, G, P),
            k_s.reshape(1, DB, DS, NK, SUB_DIM), v_s.reshape(1, DB, DS, N_HEADS, HEAD_DIM),
            sre_s.reshape(1, DB, G, P), sim_s.reshape(1, DB, G, P))
```
